```python
import math
import jax, jax.numpy as jnp
from jax import lax
import numpy as np

D_MODEL = 1024
BATCH = 2
SEQ = 16384
DEPTH = 4

N_HEADS = 16
QK_NOPE_DIM = 64
QK_ROPE_DIM = 32
QK_DIM = QK_NOPE_DIM + QK_ROPE_DIM
V_HEAD_DIM = 64
Q_LORA_RANK = 384
KV_LORA_RANK = 256
ROPE_THETA = 10000.0
Q_BLOCK = 128
HYENA_WIDTH = 1024
HYENA_ORDER = 2
SHORT_CONV_WIDTH = 3
FILTER_EMB_DIM = 33
FILTER_HIDDEN = 64
DECAY_TARGET = 1e-2
FAST_DECAY_PCT = 0.3
SLOW_DECAY_PCT = 1.5
N_EXPERTS = 16
EC_CAPACITY_FACTOR = 2
EXPERT_FF = 1408
N_BRANCHES = 2
RMS_EPS = 1e-6

COL_Q = 0
COL_KV = COL_Q + Q_LORA_RANK
COL_KROPE = COL_KV + KV_LORA_RANK
COL_HYENA = COL_KROPE + QK_ROPE_DIM
COL_GATE = COL_HYENA + (HYENA_ORDER + 1) * HYENA_WIDTH
IN_COLS = COL_GATE + N_BRANCHES * D_MODEL

kernel_name = "hybrid_mla_hyena_ec_moe_encoder"


def rms_norm(x, w):
    xf = x.astype(jnp.float32)
    y = xf * lax.rsqrt(jnp.mean(xf * xf, axis=-1, keepdims=True) + RMS_EPS)
    return y.astype(x.dtype) * w


def rope_tables(S):
    pos = jnp.arange(S, dtype=jnp.float32)
    inv_freq = ROPE_THETA ** (-jnp.arange(0, QK_ROPE_DIM, 2, dtype=jnp.float32) / QK_ROPE_DIM)
    ang = pos[:, None] * inv_freq[None, :]
    return jnp.cos(ang), jnp.sin(ang)


def apply_rope(t, cos, sin):
    half = QK_ROPE_DIM // 2
    t1, t2 = t[..., :half], t[..., half:]
    c = cos[None, :, None, :].astype(t.dtype)
    s = sin[None, :, None, :].astype(t.dtype)
    return jnp.concatenate([t1 * c - t2 * s, t1 * s + t2 * c], axis=-1)


def block_attention(q, k, v):
    B, S, H, _ = q.shape
    nb = S // Q_BLOCK
    scale = QK_DIM ** -0.5
    qb = q.reshape(B, nb, Q_BLOCK, H, QK_DIM).transpose(1, 0, 2, 3, 4)

    def one_block(q_blk):
        s = jnp.einsum('bqhd,bkhd->bhqk', q_blk, k).astype(jnp.float32) * scale
        p = jax.nn.softmax(s, axis=-1).astype(v.dtype)
        return jnp.einsum('bhqk,bkhd->bqhd', p, v)

    o = lax.map(one_block, qb)
    return o.transpose(1, 0, 2, 3, 4).reshape(B, S, H * V_HEAD_DIM)


def mla_branch(q_lat, kv_lat, k_rope, cos, sin, q_a_norm_w, w_q_b, kv_a_norm_w, w_kv_b, q_norm_w, k_norm_w):
    B, S, _ = q_lat.shape
    q = (rms_norm(q_lat, q_a_norm_w) @ w_q_b).reshape(B, S, N_HEADS, QK_DIM)
    kv = (rms_norm(kv_lat, kv_a_norm_w) @ w_kv_b).reshape(B, S, N_HEADS, QK_NOPE_DIM + V_HEAD_DIM)
    k_nope, v = kv[..., :QK_NOPE_DIM], kv[..., QK_NOPE_DIM:]
    k_rope_h = jnp.broadcast_to(k_rope[:, :, None, :], (B, S, N_HEADS, QK_ROPE_DIM))
    k = jnp.concatenate([k_nope, k_rope_h], axis=-1)
    q = rms_norm(q, q_norm_w)
    k = rms_norm(k, k_norm_w)
    q = jnp.concatenate([q[..., :QK_NOPE_DIM], apply_rope(q[..., QK_NOPE_DIM:], cos, sin)], axis=-1)
    k = jnp.concatenate([k[..., :QK_NOPE_DIM], apply_rope(k[..., QK_NOPE_DIM:], cos, sin)], axis=-1)
    return block_attention(q, k, v)


def hyena_positional_features(L):
    pos = jnp.arange(L, dtype=jnp.float32)
    t = pos / L
    bands = (FILTER_EMB_DIM - 1) // 2
    f = jnp.linspace(1e-4, bands - 1, bands, dtype=jnp.float32)
    w = 2.0 * math.pi * t[:, None] * f[None, :]
    feats = jnp.concatenate([t[:, None], jnp.cos(w), jnp.sin(w)], axis=-1)
    tau = jnp.abs(pos - (L // 2)) / max(L // 2, 1)
    return feats, tau


def implicit_filters(feats, tau, w1, b1, fr1, w2, b2, fr2, w3):
    f32 = jnp.float32
    h = jnp.sin(fr1.astype(f32) * (feats @ w1.astype(f32) + b1.astype(f32)))
    h = jnp.sin(fr2.astype(f32) * (h @ w2.astype(f32) + b2.astype(f32)))
    h = (h @ w3.astype(f32)).reshape(feats.shape[0], HYENA_ORDER, HYENA_WIDTH)
    max_decay = math.log(DECAY_TARGET) / FAST_DECAY_PCT
    min_decay = math.log(DECAY_TARGET) / SLOW_DECAY_PCT
    deltas = jnp.linspace(min_decay, max_decay, HYENA_WIDTH, dtype=f32)
    h = h * jnp.exp(-tau[:, None, None] * jnp.abs(deltas))[:, None, :] if False else h * jnp.exp(-tau[:, None, None] * jnp.abs(deltas)[None, None, :])
    return h / jnp.sum(jnp.abs(h), axis=0, keepdims=True)


def long_conv(z, h_spec, skip):
    L = z.shape[1]
    Z = jnp.fft.rfft(z.astype(jnp.float32), n=2 * L, axis=1)
    y = jnp.fft.irfft(Z * h_spec[None], n=2 * L, axis=1)[:, L // 2: L // 2 + L]
    return y.astype(z.dtype) + z * skip


def short_conv(u, w, b):
    S = u.shape[1]
    pad = SHORT_CONV_WIDTH // 2
    up = jnp.pad(u, ((0, 0), (pad, pad), (0, 0)))
    out = b
    for j in range(SHORT_CONV_WIDTH):
        out = out + up[:, j:j + S] * w[j]
    return out


def hyena_branch(u_cols, feats, tau, conv_w, conv_b, fw1, fb1, ff1, fw2, fb2, ff2, fw3, skip):
    S = u_cols.shape[1]
    u = short_conv(u_cols, conv_w, conv_b)
    parts = jnp.split(u, HYENA_ORDER + 1, axis=-1)
    h = implicit_filters(feats, tau, fw1, fb1, ff1, fw2, fb2, ff2, fw3)
    h_spec = jnp.fft.rfft(h, n=2 * S, axis=0)
    z = parts[0]
    for n in range(HYENA_ORDER):
        z = parts[n + 1] * long_conv(z, h_spec[:, n], skip[n])
    return z


def expert_choice_ffn(h, router_w, wg, wu, wd):
    B, S, D = h.shape
    C = max(1, EC_CAPACITY_FACTOR * S // N_EXPERTS)
    aff = jax.nn.softmax((h @ router_w).astype(jnp.float32), axis=-1)
    gate, idx = lax.top_k(jnp.swapaxes(aff, 1, 2), C)
    flat_idx = idx.reshape(B, N_EXPERTS * C)
    xs = jnp.take_along_axis(h, flat_idx[..., None], axis=1).reshape(B, N_EXPERTS, C, D)
    a = jnp.einsum('becd,edf->becf', xs, wg)
    u = jnp.einsum('becd,edf->becf', xs, wu)
    o = jnp.einsum('becf,efd->becd', jax.nn.silu(a) * u, wd) * gate[..., None].astype(h.dtype)
    o = o.reshape(B, N_EXPERTS * C, D)
    return jax.vmap(lambda oi, ii: jnp.zeros((S, D), h.dtype).at[ii].add(oi))(o, flat_idx)


def setup_inputs(seed: int = 0) -> dict:
    key = jax.random.key(seed)
    ks = jax.random.split(key, 32)

    def nrm(k, shape, scale):
        return jax.random.normal(k, shape, jnp.float32) * scale

    L = DEPTH
    return {
        "x": nrm(ks[0], (BATCH, SEQ, D_MODEL), 1.0),
        "norm1_w": 1.0 + nrm(ks[1], (L, D_MODEL), 0.02),
        "w_in": nrm(ks[2], (L, D_MODEL, IN_COLS), D_MODEL ** -0.5),
        "q_a_norm_w": 1.0 + nrm(ks[3], (L, Q_LORA_RANK), 0.02),
        "w_q_b": nrm(ks[4], (L, Q_LORA_RANK, N_HEADS * QK_DIM), Q_LORA_RANK ** -0.5),
        "kv_a_norm_w": 1.0 + nrm(ks[5], (L, KV_LORA_RANK), 0.02),
        "w_kv_b": nrm(ks[6], (L, KV_LORA_RANK, N_HEADS * (QK_NOPE_DIM + V_HEAD_DIM)), KV_LORA_RANK ** -0.5),
        "q_norm_w": 1.0 + nrm(ks[7], (L, QK_DIM), 0.02),
        "k_norm_w": 1.0 + nrm(ks[8], (L, QK_DIM), 0.02),
        "short_conv_w": nrm(ks[9], (L, SHORT_CONV_WIDTH, (HYENA_ORDER + 1) * HYENA_WIDTH), SHORT_CONV_WIDTH ** -0.5),
        "short_conv_b": nrm(ks[10], (L, (HYENA_ORDER + 1) * HYENA_WIDTH), 0.02),
        "filt_w1": nrm(ks[11], (L, FILTER_EMB_DIM, FILTER_HIDDEN), FILTER_EMB_DIM ** -0.5),
        "filt_b1": nrm(ks[12], (L, FILTER_HIDDEN), 0.1),
        "filt_freq1": 1.0 + nrm(ks[13], (L, FILTER_HIDDEN), 0.02),
        "filt_w2": nrm(ks[14], (L, FILTER_HIDDEN, FILTER_HIDDEN), FILTER_HIDDEN ** -0.5),
        "filt_b2": nrm(ks[15], (L, FILTER_HIDDEN), 0.1),
        "filt_freq2": 1.0 + nrm(ks[16], (L, FILTER_HIDDEN), 0.02),
        "filt_w3": nrm(ks[17], (L, FILTER_HIDDEN, HYENA_ORDER * HYENA_WIDTH), FILTER_HIDDEN ** -0.5),
        "filt_skip": nrm(ks[18], (L, HYENA_ORDER, HYENA_WIDTH), 0.5),
        "w_attn_branch": nrm(ks[19], (L, N_HEADS * V_HEAD_DIM, D_MODEL), (N_HEADS * V_HEAD_DIM) ** -0.5),
        "w_hyena_branch": nrm(ks[20], (L, HYENA_WIDTH, D_MODEL), HYENA_WIDTH ** -0.5),
        "w_out": nrm(ks[21], (L, D_MODEL, D_MODEL), D_MODEL ** -0.5),
        "norm2_w": 1.0 + nrm(ks[22], (L, D_MODEL), 0.02),
        "router_w": nrm(ks[23], (L, D_MODEL, N_EXPERTS), D_MODEL ** -0.5),
        "expert_w_gate": nrm(ks[24], (L, N_EXPERTS, D_MODEL, EXPERT_FF), D_MODEL ** -0.5),
        "expert_w_up": nrm(ks[25], (L, N_EXPERTS, D_MODEL, EXPERT_FF), D_MODEL ** -0.5),
        "expert_w_down": nrm(ks[26], (L, N_EXPERTS, EXPERT_FF, D_MODEL), EXPERT_FF ** -0.5),
    }


def reference(x, norm1_w, w_in, q_a_norm_w, w_q_b, kv_a_norm_w, w_kv_b, q_norm_w, k_norm_w,
              short_conv_w, short_conv_b, filt_w1, filt_b1, filt_freq1, filt_w2, filt_b2, filt_freq2,
              filt_w3, filt_skip, w_attn_branch, w_hyena_branch, w_out, norm2_w, router_w,
              expert_w_gate, expert_w_up, expert_w_down):
    B, S, D = x.shape
    cos, sin = rope_tables(S)
    feats, tau = hyena_positional_features(S)
    for l in range(DEPTH):
        xn = rms_norm(x, norm1_w[l])
        cols = xn @ w_in[l]
        attn = mla_branch(cols[..., COL_Q:COL_KV], cols[..., COL_KV:COL_KROPE],
                          cols[..., COL_KROPE:COL_HYENA], cos, sin,
                          q_a_norm_w[l], w_q_b[l], kv_a_norm_w[l], w_kv_b[l], q_norm_w[l], k_norm_w[l])
        hy = hyena_branch(cols[..., COL_HYENA:COL_GATE], feats, tau, short_conv_w[l], short_conv_b[l],
                          filt_w1[l], filt_b1[l], filt_freq1[l], filt_w2[l], filt_b2[l], filt_freq2[l],
                          filt_w3[l], filt_skip[l])
        gates = jax.nn.sigmoid(cols[..., COL_GATE:].astype(jnp.float32)).astype(x.dtype)
        gates = gates.reshape(B, S, N_BRANCHES, D)
        merged = gates[:, :, 0] * (attn @ w_attn_branch[l]) + gates[:, :, 1] * (hy @ w_hyena_branch[l])
        x = x + merged @ w_out[l]
        x = x + expert_choice_ffn(rms_norm(x, norm2_w[l]), router_w[l],
                                  expert_w_gate[l], expert_w_up[l], expert_w_down[l])
    return x
```

```python
import functools
import math

import jax
import jax.numpy as jnp
from jax import lax
from jax.experimental import pallas as pl
from jax.experimental.pallas import tpu as pltpu

f32 = jnp.float32
bf16 = jnp.bfloat16

N_HEADS = 16
QK_NOPE = 64
QK_ROPE = 32
QK_DIM = QK_NOPE + QK_ROPE
V_DIM = 64
Q_LORA = 384
KV_LORA = 256
ROPE_THETA = 10000.0
HY_W = 1024
HY_ORDER = 2
FILT_EMB = 33
FILT_HID = 64
DECAY_TARGET = 1e-2
FAST_DECAY = 0.3
SLOW_DECAY = 1.5
N_EXPERTS = 16
EC_FACTOR = 2
RMS_EPS = 1e-6

COL_Q = 0
COL_KV = COL_Q + Q_LORA
COL_KROPE = COL_KV + KV_LORA
COL_HYENA = COL_KROPE + QK_ROPE
COL_GATE = COL_HYENA + (HY_ORDER + 1) * HY_W

LANES = 128
SUBLANES = 8
DFT_MINOR = 128
GATHER_WIN = 144
VMEM_LIMIT = 56 * 1024 * 1024


def _cp(sem, vmem=VMEM_LIMIT):
    return pltpu.CompilerParams(dimension_semantics=sem, vmem_limit_bytes=vmem)


def _tile(n, pref):
    t = min(n, pref)
    assert n % t == 0, (n, pref)
    return t


def _dot(a, b):
    return jnp.dot(a, b, preferred_element_type=f32)


def _dot_hi(a, b):
    return jnp.dot(a, b, preferred_element_type=f32, precision=lax.Precision.HIGHEST)


def _in_proj_kernel(x_ref, nw_ref, wa_ref, wu_ref, wg_ref, a_ref, u_ref, g_ref):
    x = x_ref[...]
    ms = jnp.mean(x * x, axis=-1, keepdims=True)
    xn = (x * lax.rsqrt(ms + RMS_EPS) * nw_ref[...]).astype(bf16)
    a_ref[...] = _dot(xn, wa_ref[...])
    u_ref[...] = _dot(xn, wu_ref[...]).astype(bf16)
    g_ref[...] = jax.nn.sigmoid(_dot(xn, wg_ref[...])).astype(bf16)


def _in_proj(x2, nw, wa, wu, wg):
    T, D = x2.shape
    tm = _tile(T, 256)
    na, nu, ng = wa.shape[1], wu.shape[1], wg.shape[1]
    full = lambda i: (0, 0)
    row = lambda i: (i, 0)
    return pl.pallas_call(
        _in_proj_kernel,
        grid=(T // tm,),
        in_specs=[pl.BlockSpec((tm, D), row), pl.BlockSpec((1, D), full),
                  pl.BlockSpec((D, na), full), pl.BlockSpec((D, nu), full), pl.BlockSpec((D, ng), full)],
        out_specs=[pl.BlockSpec((tm, na), row), pl.BlockSpec((tm, nu), row), pl.BlockSpec((tm, ng), row)],
        out_shape=[jax.ShapeDtypeStruct((T, na), f32), jax.ShapeDtypeStruct((T, nu), bf16),
                   jax.ShapeDtypeStruct((T, ng), bf16)],
        compiler_params=_cp(("parallel",)),
        name="in_proj",
    )(x2, nw, wa, wu, wg)


def _head_norm_rope(t, w, ct, s1, s2):
    ss = jnp.sum(t * t, axis=-1, keepdims=True) * (1.0 / QK_DIM)
    t = t * lax.rsqrt(ss + RMS_EPS) * w
    return t * ct + pltpu.roll(t, LANES - QK_ROPE // 2, 1) * s1 + pltpu.roll(t, QK_ROPE // 2, 1) * s2


def _mla_prep_kernel(a_ref, qaw_ref, kvaw_ref, wq_ref, wk_ref, wv_ref, qnw_ref, knw_ref,
                     ct_ref, s1_ref, s2_ref, q_ref, k_ref, v_ref, *, qscale):
    a = a_ref[...]
    ql = a[:, :Q_LORA]
    kvl = a[:, Q_LORA:Q_LORA + KV_LORA]
    kr = a[:, Q_LORA + KV_LORA:]
    qn = (ql * lax.rsqrt(jnp.mean(ql * ql, axis=-1, keepdims=True) + RMS_EPS) * qaw_ref[...]).astype(bf16)
    kvn = (kvl * lax.rsqrt(jnp.mean(kvl * kvl, axis=-1, keepdims=True) + RMS_EPS) * kvaw_ref[...]).astype(bf16)
    q = _dot(qn, wq_ref[...])
    k = _dot(kvn, wk_ref[...])
    v = _dot(kvn, wv_ref[...])
    ct, s1, s2 = ct_ref[...], s1_ref[...], s2_ref[...]
    lane = lax.broadcasted_iota(jnp.int32, (1, LANES), 1)
    ones_col = (lane == V_DIM).astype(f32)
    for h in range(N_HEADS):
        sl = slice(h * LANES, (h + 1) * LANES)
        qh = _head_norm_rope(q[:, sl], qnw_ref[...], ct, s1, s2) * qscale
        kh = _head_norm_rope(k[:, sl] + kr, knw_ref[...], ct, s1, s2)
        q_ref[h] = qh.astype(bf16)
        k_ref[h] = kh.astype(bf16)
        v_ref[h] = (v[:, sl] + ones_col).astype(bf16)


def _mla_prep(a3, qaw, kvaw, wq, wk, wv, qnw, knw, ct, s1, s2):
    B, S, NA = a3.shape
    tm = _tile(S, 256)
    HP = N_HEADS * LANES
    qscale = (QK_DIM ** -0.5) * math.log2(math.e)
    full2 = lambda b, i: (0, 0)
    pos = lambda b, i: (i, 0)
    out_spec = pl.BlockSpec((None, N_HEADS, tm, LANES), lambda b, i: (b, 0, i, 0))
    out_sds = jax.ShapeDtypeStruct((B, N_HEADS, S, LANES), bf16)
    return pl.pallas_call(
        functools.partial(_mla_prep_kernel, qscale=qscale),
        grid=(B, S // tm),
        in_specs=[pl.BlockSpec((None, tm, NA), lambda b, i: (b, i, 0)),
                  pl.BlockSpec((1, Q_LORA), full2), pl.BlockSpec((1, KV_LORA), full2),
                  pl.BlockSpec((Q_LORA, HP), full2), pl.BlockSpec((KV_LORA, HP), full2),
                  pl.BlockSpec((KV_LORA, HP), full2),
                  pl.BlockSpec((1, LANES), full2), pl.BlockSpec((1, LANES), full2),
                  pl.BlockSpec((tm, LANES), pos), pl.BlockSpec((tm, LANES), pos), pl.BlockSpec((tm, LANES), pos)],
        out_specs=[out_spec, out_spec, out_spec],
        out_shape=[out_sds, out_sds, out_sds],
        compiler_params=_cp(("parallel", "parallel")),
        name="mla_prep",
    )(a3, qaw, kvaw, wq, wk, wv, qnw, knw, ct, s1, s2)


def _flash_kernel(q_ref, k_ref, v_ref, o_ref, m_sc, acc_sc):
    kv = pl.program_id(3)

    @pl.when(kv == 0)
    def _():
        m_sc[...] = jnp.full(m_sc.shape, -jnp.inf, f32)
        acc_sc[...] = jnp.zeros(acc_sc.shape, f32)

    s = lax.dot_general(q_ref[...], k_ref[...], (((1,), (1,)), ((), ())), preferred_element_type=f32)
    m_prev = m_sc[...]
    m_new = jnp.maximum(m_prev, jnp.max(s, axis=1, keepdims=True))
    alpha = jnp.exp2(m_prev - m_new)
    p = jnp.exp2(s - m_new)
    acc_sc[...] = alpha * acc_sc[...] + _dot(p.astype(bf16), v_ref[...])
    m_sc[...] = m_new

    @pl.when(kv == pl.num_programs(3) - 1)
    def _():
        acc = acc_sc[...]
        o_ref[...] = (acc[:, :V_DIM] / acc[:, V_DIM:V_DIM + 1]).astype(o_ref.dtype)


def _flash(q, k, v):
    B, H, S, _ = q.shape
    tq = _tile(S, 512)
    tk = _tile(S, 512)
    return pl.pallas_call(
        _flash_kernel,
        grid=(B, H, S // tq, S // tk),
        in_specs=[pl.BlockSpec((None, None, tq, LANES), lambda b, h, i, j: (b, h, i, 0)),
                  pl.BlockSpec((None, None, tk, LANES), lambda b, h, i, j: (b, h, j, 0)),
                  pl.BlockSpec((None, None, tk, LANES), lambda b, h, i, j: (b, h, j, 0))],
        out_specs=pl.BlockSpec((None, None, tq, V_DIM), lambda b, h, i, j: (b, h, i, 0)),
        out_shape=jax.ShapeDtypeStruct((B, H, S, V_DIM), bf16),
        scratch_shapes=[pltpu.VMEM((tq, 1), f32), pltpu.VMEM((tq, LANES), f32)],
        compiler_params=_cp(("parallel", "parallel", "parallel", "arbitrary")),
        name="flash_attn",
    )(q, k, v)


def _short_conv_kernel(u_ref, prev_ref, next_ref, w_ref, b_ref, o_ref):
    j = pl.program_id(1)
    last = pl.num_programs(1) - 1
    cur = u_ref[...].astype(f32)
    prev = prev_ref[...].astype(f32)
    nxt = next_ref[...].astype(f32)
    nh = cur.shape[0]
    rows = lax.broadcasted_iota(jnp.int32, (nh, 1), 0)
    prev_wrap = jnp.where(rows == 0, 0.0, pltpu.roll(prev, 1, 0))
    next_wrap = jnp.where(rows == nh - 1, 0.0, pltpu.roll(nxt, nh - 1, 0))
    up = jnp.where(j == 0, prev_wrap, prev)
    dn = jnp.where(j == last, next_wrap, nxt)
    w = w_ref[...]
    o_ref[...] = (b_ref[...] + up * w[0:1, :] + cur * w[1:2, :] + dn * w[2:3, :]).astype(o_ref.dtype)


def _short_conv(ut, w, b):
    B, P, nh, CT = ut.shape
    tc = _tile(CT, 1024)
    plane = lambda off: pl.BlockSpec((None, None, nh, tc), lambda bi, j, c: (bi, (j + off + P) % P, 0, c))
    return pl.pallas_call(
        _short_conv_kernel,
        grid=(B, P, CT // tc),
        in_specs=[plane(0), plane(-1), plane(1),
                  pl.BlockSpec((3, tc), lambda bi, j, c: (0, c)),
                  pl.BlockSpec((1, tc), lambda bi, j, c: (0, c))],
        out_specs=plane(0),
        out_shape=jax.ShapeDtypeStruct((B, P, nh, CT), bf16),
        compiler_params=_cp(("parallel", "parallel", "parallel")),
        name="short_conv",
    )(ut, ut, ut, w, b)


def _filter_kernel(feat_ref, tau_ref, w1_ref, b1_ref, f1_ref, w2_ref, b2_ref, f2_ref, w3_ref, dl_ref,
                   h_ref, l1_ref):
    i = pl.program_id(0)
    h = jnp.sin(f1_ref[...] * (_dot_hi(feat_ref[...], w1_ref[...]) + b1_ref[...]))
    h = jnp.sin(f2_ref[...] * (_dot_hi(h, w2_ref[...]) + b2_ref[...]))
    h = _dot_hi(h, w3_ref[...])
    h = h * jnp.exp(-tau_ref[...] * dl_ref[...])
    h_ref[...] = h

    @pl.when(i == 0)
    def _():
        l1_ref[...] = jnp.zeros(l1_ref.shape, f32)

    l1_ref[...] += jnp.sum(jnp.abs(h), axis=0, keepdims=True)


def _filters(feats, tau, w1, b1, f1, w2, b2, f2, w3, dl):
    S = feats.shape[0]
    CO = w3.shape[1]
    tm = _tile(S, 512)
    full = lambda i: (0, 0)
    row = lambda i: (i, 0)
    return pl.pallas_call(
        _filter_kernel,
        grid=(S // tm,),
        in_specs=[pl.BlockSpec((tm, LANES), row), pl.BlockSpec((tm, 1), row),
                  pl.BlockSpec((LANES, LANES), full), pl.BlockSpec((1, LANES), full), pl.BlockSpec((1, LANES), full),
                  pl.BlockSpec((LANES, LANES), full), pl.BlockSpec((1, LANES), full), pl.BlockSpec((1, LANES), full),
                  pl.BlockSpec((LANES, CO), full), pl.BlockSpec((1, CO), full)],
        out_specs=[pl.BlockSpec((tm, CO), row), pl.BlockSpec((1, CO), full)],
        out_shape=[jax.ShapeDtypeStruct((S, CO), f32), jax.ShapeDtypeStruct((1, CO), f32)],
        compiler_params=_cp(("arbitrary",)),
        name="hyena_filters",
    )(feats, tau, w1, b1, f1, w2, b2, f2, w3, dl)


def _dft_a_kernel(x_ref, m_ref, tc_ref, ts_ref, a_ref, *, packed):
    n1f = m_ref.shape[0] // 2
    m = m_ref[...]
    if packed:
        pr = _dot(m, x_ref[0].astype(bf16))
        pi = _dot(m, x_ref[1].astype(bf16))
        ar = pr[:n1f] + pi[n1f:]
        ai = pi[:n1f] - pr[n1f:]
    else:
        p = _dot(m, x_ref[...].astype(bf16))
        ar = p[:n1f]
        ai = -p[n1f:]
    tc, ts = tc_ref[...], ts_ref[...]
    a_ref[0] = (ar * tc + ai * ts).astype(a_ref.dtype)
    a_ref[1] = (ai * tc - ar * ts).astype(a_ref.dtype)


def _dft_a(x4, col_blk, C, m1, twc, tws, packed):
    n1f = m1.shape[0] // 2
    nh = m1.shape[1]
    if packed:
        x_spec = pl.BlockSpec((2, None, nh, C), lambda j: (0, j, 0, col_blk))
    else:
        x_spec = pl.BlockSpec((None, nh, C), lambda j: (j, 0, col_blk))
    return pl.pallas_call(
        functools.partial(_dft_a_kernel, packed=packed),
        grid=(DFT_MINOR,),
        in_specs=[x_spec,
                  pl.BlockSpec((2 * n1f, nh), lambda j: (0, 0)),
                  pl.BlockSpec((None, n1f, 1), lambda j: (j, 0, 0)),
                  pl.BlockSpec((None, n1f, 1), lambda j: (j, 0, 0))],
        out_specs=pl.BlockSpec((2, None, n1f, C), lambda j: (0, j, 0, 0)),
        out_shape=jax.ShapeDtypeStruct((2, DFT_MINOR, n1f, C), bf16),
        compiler_params=_cp(("parallel",)),
        name="dft_stage_a",
    )(x4, m1, twc, tws)


def _dft_bf_kernel(a_ref, m2_ref, inv_ref, h_ref):
    C = a_ref.shape[-1]
    a = a_ref[...].reshape(2 * DFT_MINOR, C)
    z = _dot(m2_ref[...], a) * inv_ref[...]
    h_ref[...] = z.reshape(2, DFT_MINOR, C).astype(h_ref.dtype)


def _dft_b_filter(a4, m2f, inv_l1):
    _, _, n1f, C = a4.shape
    a4 = a4.swapaxes(1, 2)
    return pl.pallas_call(
        _dft_bf_kernel,
        grid=(n1f,),
        in_specs=[pl.BlockSpec((2, None, DFT_MINOR, C), lambda k: (0, k, 0, 0)),
                  pl.BlockSpec((2 * DFT_MINOR, 2 * DFT_MINOR), lambda k: (0, 0)),
                  pl.BlockSpec((1, C), lambda k: (0, 0))],
        out_specs=pl.BlockSpec((None, 2, DFT_MINOR, C), lambda k: (k, 0, 0, 0)),
        out_shape=jax.ShapeDtypeStruct((n1f, 2, DFT_MINOR, C), bf16),
        compiler_params=_cp(("parallel",)),
        name="dft_stage_b_filter",
    )(a4, m2f, inv_l1)


def _dft_b_kernel(a_ref, h_ref, m2f_ref, m2i_ref, tc_ref, ts_ref, b_ref):
    C = a_ref.shape[-1]
    a = a_ref[...].reshape(2 * DFT_MINOR, C)
    z = _dot(m2f_ref[...], a)
    zr, zi = z[:DFT_MINOR], z[DFT_MINOR:]
    hr, hi = h_ref[0].astype(f32), h_ref[1].astype(f32)
    y = jnp.concatenate([zr * hr - zi * hi, zr * hi + zi * hr], axis=0).astype(bf16)
    w = _dot(m2i_ref[...], y)
    wr, wi = w[:DFT_MINOR], w[DFT_MINOR:]
    tc, ts = tc_ref[...], ts_ref[...]
    b_ref[0] = (wr * tc - wi * ts).astype(b_ref.dtype)
    b_ref[1] = (wr * ts + wi * tc).astype(b_ref.dtype)


def _dft_b(a4, hspec, order, m2f, m2i, twc_t, tws_t):
    _, _, n1f, C = a4.shape
    a4 = a4.swapaxes(1, 2)
    return pl.pallas_call(
        _dft_b_kernel,
        grid=(n1f,),
        in_specs=[pl.BlockSpec((2, None, DFT_MINOR, C), lambda k: (0, k, 0, 0)),
                  pl.BlockSpec((None, 2, DFT_MINOR, C), lambda k: (k, 0, 0, order)),
                  pl.BlockSpec((2 * DFT_MINOR, 2 * DFT_MINOR), lambda k: (0, 0)),
                  pl.BlockSpec((2 * DFT_MINOR, 2 * DFT_MINOR), lambda k: (0, 0)),
                  pl.BlockSpec((None, DFT_MINOR, 1), lambda k: (k, 0, 0)),
                  pl.BlockSpec((None, DFT_MINOR, 1), lambda k: (k, 0, 0))],
        out_specs=pl.BlockSpec((2, None, DFT_MINOR, C), lambda k: (0, k, 0, 0)),
        out_shape=jax.ShapeDtypeStruct((2, n1f, DFT_MINOR, C), bf16),
        compiler_params=_cp(("parallel",)),
        name="dft_stage_b",
    )(a4, hspec, m2f, m2i, twc_t, tws_t)


def _dft_c_kernel(b_ref, g_ref, z_ref, xg_ref, skip_ref, o_ref):
    n1f = b_ref.shape[1]
    C = b_ref.shape[-1]
    nh = n1f // 2
    bb = b_ref[...].reshape(2 * n1f, C)
    y = _dot(g_ref[...], bb)
    skip = skip_ref[...]
    for bi in range(2):
        z = z_ref[bi].astype(f32)
        o_ref[bi] = (xg_ref[bi].astype(f32) * (y[bi * nh:(bi + 1) * nh] + z * skip)).astype(o_ref.dtype)


def _dft_c(b4, gm, z4, z_blk, xg4, xg_blk, skip):
    _, n1f, _, C = b4.shape
    nh = n1f // 2
    b4 = b4.swapaxes(1, 2)
    return pl.pallas_call(
        _dft_c_kernel,
        grid=(DFT_MINOR,),
        in_specs=[pl.BlockSpec((2, None, n1f, C), lambda j: (0, j, 0, 0)),
                  pl.BlockSpec((n1f, 2 * n1f), lambda j: (0, 0)),
                  pl.BlockSpec((2, None, nh, C), lambda j: (0, j, 0, z_blk)),
                  pl.BlockSpec((2, None, nh, C), lambda j: (0, j, 0, xg_blk)),
                  pl.BlockSpec((1, C), lambda j: (0, 0))],
        out_specs=pl.BlockSpec((2, None, nh, C), lambda j: (0, j, 0, 0)),
        out_shape=jax.ShapeDtypeStruct((2, DFT_MINOR, nh, C), bf16),
        compiler_params=_cp(("parallel",)),
        name="dft_stage_c",
    )(b4, gm, z4, xg4, skip)


def _dft_tables(S):
    N = 2 * S
    n1f = N // DFT_MINOR
    nh = n1f // 2
    two_pi = 2.0 * math.pi

    def cs(num, den):
        ang = (num % den).astype(f32) * (two_pi / den)
        return jnp.cos(ang), jnp.sin(ang)

    k1 = jnp.arange(n1f, dtype=jnp.int32)
    n1 = jnp.arange(nh, dtype=jnp.int32)
    c1, s1 = cs(k1[:, None] * n1[None, :], n1f)
    m1 = jnp.concatenate([c1, s1], axis=0).astype(bf16)
    n2 = jnp.arange(DFT_MINOR, dtype=jnp.int32)
    twc, tws = cs(n2[:, None] * k1[None, :], N)
    c2, s2 = cs(n2[:, None] * n2[None, :], DFT_MINOR)
    m2f = jnp.concatenate([jnp.concatenate([c2, s2], 1), jnp.concatenate([-s2, c2], 1)], 0).astype(bf16)
    m2i = jnp.concatenate([jnp.concatenate([c2, -s2], 1), jnp.concatenate([s2, c2], 1)], 0).astype(bf16)
    rows = jnp.arange(nh, dtype=jnp.int32) + n1f // 4
    gc, gs = cs(rows[:, None] * k1[None, :], n1f)
    gm = (jnp.concatenate([jnp.concatenate([gc, -gs], 1), jnp.concatenate([gs, gc], 1)], 0) / N).astype(bf16)
    return dict(m1=m1, twc=twc[:, :, None], tws=tws[:, :, None],
                twc_t=twc.T[:, :, None], tws_t=tws.T[:, :, None], m2f=m2f, m2i=m2i, gm=gm)


def _merge_kernel(x_ref, at_ref, hy_ref, g_ref, wa_ref, wh_ref, wo_ref, n2_ref, wr_ref,
                  xo_ref, hn_ref, aff_ref):
    D = x_ref.shape[1]
    g = g_ref[...].astype(f32)
    merged = g[:, :D] * _dot(at_ref[...], wa_ref[...]) + g[:, D:] * _dot(hy_ref[...], wh_ref[...])
    x = x_ref[...] + _dot(merged.astype(bf16), wo_ref[...])
    xo_ref[...] = x
    hn = x * lax.rsqrt(jnp.mean(x * x, axis=-1, keepdims=True) + RMS_EPS) * n2_ref[...]
    hn_ref[...] = hn.astype(bf16)
    logits = _dot_hi(hn, wr_ref[...])
    lane = lax.broadcasted_iota(jnp.int32, logits.shape, 1)
    logits = jnp.where(lane < N_EXPERTS, logits, -jnp.inf)
    e = jnp.exp(logits - jnp.max(logits, axis=-1, keepdims=True))
    aff_ref[...] = e / jnp.sum(e, axis=-1, keepdims=True)


def _merge(x2, attn, hy, g, wa, wh, wo, n2w, wr):
    T, D = x2.shape
    tm = _tile(T, 256)
    row = lambda i: (i, 0)
    full = lambda i: (0, 0)
    return pl.pallas_call(
        _merge_kernel,
        grid=(T // tm,),
        in_specs=[pl.BlockSpec((tm, D), row), pl.BlockSpec((tm, D), row), pl.BlockSpec((tm, D), row),
                  pl.BlockSpec((tm, 2 * D), row),
                  pl.BlockSpec((D, D), full), pl.BlockSpec((D, D), full), pl.BlockSpec((D, D), full),
                  pl.BlockSpec((1, D), full), pl.BlockSpec((D, LANES), full)],
        out_specs=[pl.BlockSpec((tm, D), row), pl.BlockSpec((tm, D), row), pl.BlockSpec((tm, LANES), row)],
        out_shape=[jax.ShapeDtypeStruct((T, D), f32), jax.ShapeDtypeStruct((T, D), bf16),
                   jax.ShapeDtypeStruct((T, LANES), f32)],
        compiler_params=_cp(("parallel",)),
        name="merge_out_router",
    )(x2, attn, hy, g, wa, wh, wo, n2w, wr)


def _thresh_kernel(a_ref, thr_ref, need_ref, *, cap):
    bits = pltpu.bitcast(a_ref[...], jnp.int32)
    R = bits.shape[0]

    def body(i, thr):
        cand = thr | jnp.left_shift(jnp.int32(1), 30 - i)
        cnt = jnp.sum((bits >= cand).astype(f32), axis=1, keepdims=True)
        return jnp.where(cnt >= cap, cand, thr)

    thr = lax.fori_loop(0, 31, body, jnp.zeros((R, 1), jnp.int32))
    n_gt = jnp.sum((bits > thr).astype(f32), axis=1, keepdims=True)
    thr_ref[...] = jnp.broadcast_to(pltpu.bitcast(thr, f32), thr_ref.shape)
    need_ref[...] = jnp.broadcast_to(cap - n_gt, need_ref.shape)


def _thresholds(aff_t, cap):
    R, S = aff_t.shape
    return pl.pallas_call(
        functools.partial(_thresh_kernel, cap=float(cap)),
        out_shape=[jax.ShapeDtypeStruct((R, LANES), f32), jax.ShapeDtypeStruct((R, LANES), f32)],
        compiler_params=_cp(None),
        name="route_threshold",
    )(aff_t)


def _route_kernel(aff_ref, thr_ref, need_ref, ltri_ref, slab_ref, sel_ref, pos_ref, base_ref, ceq_sc, csel_sc):
    k = pl.program_id(1)

    @pl.when(k == 0)
    def _():
        ceq_sc[...] = jnp.zeros(ceq_sc.shape, f32)
        csel_sc[...] = jnp.zeros(csel_sc.shape, f32)

    a = aff_ref[...]
    thr = thr_ref[...]
    gt = a > thr
    eq = a == thr
    eqf = eq.astype(f32)
    ltri = ltri_ref[...]
    eq_rank = ceq_sc[...] + _dot(ltri, eqf.astype(bf16))
    sel = jnp.logical_or(gt, jnp.logical_and(eq, eq_rank < need_ref[...]))
    self_ = sel.astype(f32)
    pos = csel_sc[...] + _dot(ltri, self_.astype(bf16))
    sel_ref[...] = self_
    pos_ref[...] = pos
    base_ref[...] = csel_sc[...] + _dot(slab_ref[...], self_.astype(bf16))
    ceq_sc[...] += jnp.sum(eqf, axis=0, keepdims=True)
    csel_sc[...] += jnp.sum(self_, axis=0, keepdims=True)


def _route(aff3, thr3, need3):
    B, S, _ = aff3.shape
    tk = _tile(S, 1024)
    ltri = (jnp.arange(tk)[:, None] > jnp.arange(tk)[None, :]).astype(bf16)
    tok = pl.BlockSpec((None, tk, LANES), lambda b, k: (b, k, 0))
    per_b = pl.BlockSpec((None, 1, LANES), lambda b, k: (b, 0, 0))
    return pl.pallas_call(
        _route_kernel,
        grid=(B, S // tk),
        in_specs=[tok, per_b, per_b, pl.BlockSpec((tk, tk), lambda b, k: (0, 0)),
                  pl.BlockSpec((tk // LANES, tk), lambda b, k: (0, 0))],
        out_specs=[tok, tok, pl.BlockSpec((None, tk // LANES, LANES), lambda b, k: (b, k, 0))],
        out_shape=[jax.ShapeDtypeStruct((B, S, LANES), f32), jax.ShapeDtypeStruct((B, S, LANES), f32),
                   jax.ShapeDtypeStruct((B, S // LANES, LANES), f32)],
        scratch_shapes=[pltpu.VMEM((1, LANES), f32), pltpu.VMEM((1, LANES), f32)],
        compiler_params=_cp(("parallel", "arbitrary")),
        name="route_positions",
    )(aff3, thr3, need3, ltri, ltri[::LANES])


def _gather_kernel(base_sm, hn_ref, pos_ref, sel_ref, gate_ref, xs_ref, gs_ref, acc_sc, gacc_sc, *, nsub, nsub_tot):
    b, e, k = pl.program_id(0), pl.program_id(1), pl.program_id(2)
    cap = xs_ref.shape[0]

    @pl.when(k == 0)
    def _():
        acc_sc[...] = jnp.zeros(acc_sc.shape, f32)
        gacc_sc[...] = jnp.zeros(gacc_sc.shape, f32)

    rows = lax.broadcasted_iota(jnp.int32, (GATHER_WIN, LANES), 0).astype(f32)
    for j in range(nsub):
        base = base_sm[(b * N_EXPERTS + e) * nsub_tot + k * nsub + j]
        start = pl.multiple_of((base // SUBLANES) * SUBLANES, SUBLANES)
        sl = slice(j * LANES, (j + 1) * LANES)
        rel = pos_ref[:, sl] - start.astype(f32)
        hit = jnp.logical_and(rows == rel, sel_ref[:, sl] > 0.0)
        oht = jnp.where(hit, 1.0, 0.0)
        acc_sc[pl.ds(start, GATHER_WIN), :] += _dot(oht.astype(bf16), hn_ref[sl, :])
        gacc_sc[pl.ds(start, GATHER_WIN), :] += jnp.sum(oht * gate_ref[:, sl], axis=1, keepdims=True)

    @pl.when(k == pl.num_programs(2) - 1)
    def _():
        xs_ref[...] = acc_sc[0:cap, :].astype(xs_ref.dtype)
        gs_ref[...] = gacc_sc[0:cap, :]


def _gather(base_flat, hn3, pos_t, sel_t, gate_t, cap):
    B, S, D = hn3.shape
    tk = _tile(S, 1024)
    nsub = tk // LANES
    row = pl.BlockSpec((None, None, 1, tk), lambda b, e, k, base: (b, e, 0, k))
    return pl.pallas_call(
        functools.partial(_gather_kernel, nsub=nsub, nsub_tot=S // LANES),
        grid_spec=pltpu.PrefetchScalarGridSpec(
            num_scalar_prefetch=1,
            grid=(B, N_EXPERTS, S // tk),
            in_specs=[pl.BlockSpec((None, tk, D), lambda b, e, k, base: (b, k, 0)), row, row, row],
            out_specs=[pl.BlockSpec((None, None, cap, D), lambda b, e, k, base: (b, e, 0, 0)),
                       pl.BlockSpec((None, None, cap, 1), lambda b, e, k, base: (b, e, 0, 0))],
            scratch_shapes=[pltpu.VMEM((cap + GATHER_WIN, D), f32), pltpu.VMEM((cap + GATHER_WIN, 1), f32)]),
        out_shape=[jax.ShapeDtypeStruct((B, N_EXPERTS, cap, D), bf16),
                   jax.ShapeDtypeStruct((B, N_EXPERTS, cap, 1), f32)],
        compiler_params=_cp(("parallel", "parallel", "arbitrary")),
        name="moe_gather",
    )(base_flat, hn3, pos_t, sel_t, gate_t)


def _ffn_kernel(xs_ref, gs_ref, wg_ref, wu_ref, wd_ref, o_ref):
    xs = xs_ref[...]
    a = _dot(xs, wg_ref[...])
    u = _dot(xs, wu_ref[...])
    hmid = (a * jax.nn.sigmoid(a) * u).astype(bf16)
    o_ref[...] = (_dot(hmid, wd_ref[...]) * gs_ref[...]).astype(o_ref.dtype)


def _ffn(xs, gs, wg, wu, wd):
    B, E, cap, D = xs.shape
    F = wg.shape[-1]
    tm = _tile(cap, 512)
    return pl.pallas_call(
        _ffn_kernel,
        grid=(E, B, cap // tm),
        in_specs=[pl.BlockSpec((None, None, tm, D), lambda e, b, i: (b, e, i, 0)),
                  pl.BlockSpec((None, None, tm, 1), lambda e, b, i: (b, e, i, 0)),
                  pl.BlockSpec((None, D, F), lambda e, b, i: (e, 0, 0)),
                  pl.BlockSpec((None, D, F), lambda e, b, i: (e, 0, 0)),
                  pl.BlockSpec((None, F, D), lambda e, b, i: (e, 0, 0))],
        out_specs=pl.BlockSpec((None, None, tm, D), lambda e, b, i: (b, e, i, 0)),
        out_shape=jax.ShapeDtypeStruct((B, E, cap, D), bf16),
        compiler_params=_cp(("parallel", "parallel", "parallel")),
        name="expert_ffn",
    )(xs, gs, wg, wu, wd)


def _scatter_kernel(base_sm, x_ref, pos_ref, sel_ref, o_hbm, out_ref, buf, sem, *, nsub_tot, cap):
    b, k = pl.program_id(0), pl.program_id(1)

    def window(e):
        base = base_sm[(b * N_EXPERTS + e) * nsub_tot + k]
        start = jnp.minimum((base // SUBLANES) * SUBLANES, cap - GATHER_WIN)
        start = pl.multiple_of(start, SUBLANES)
        return start, pltpu.make_async_copy(o_hbm.at[b, e, pl.ds(start, GATHER_WIN), :], buf.at[e], sem.at[e])

    for e in range(N_EXPERTS):
        window(e)[1].start()
    acc = x_ref[...]
    pos = pos_ref[...]
    sel = sel_ref[...]
    lanes = lax.broadcasted_iota(jnp.int32, (LANES, GATHER_WIN), 1).astype(f32)
    for e in range(N_EXPERTS):
        start, cp = window(e)
        cp.wait()
        rel = pos[:, e:e + 1] - start.astype(f32)
        hit = jnp.logical_and(lanes == rel, sel[:, e:e + 1] > 0.0)
        acc = acc + _dot(jnp.where(hit, 1.0, 0.0).astype(bf16), buf[e])
    out_ref[...] = acc


def _scatter(base_flat, x3, pos3, sel3, o4):
    B, S, D = x3.shape
    cap = o4.shape[2]
    tok = lambda b, k, base: (b, k, 0)
    return pl.pallas_call(
        functools.partial(_scatter_kernel, nsub_tot=S // LANES, cap=cap),
        grid_spec=pltpu.PrefetchScalarGridSpec(
            num_scalar_prefetch=1,
            grid=(B, S // LANES),
            in_specs=[pl.BlockSpec((None, LANES, D), tok), pl.BlockSpec((None, LANES, LANES), tok),
                      pl.BlockSpec((None, LANES, LANES), tok), pl.BlockSpec(memory_space=pl.ANY)],
            out_specs=pl.BlockSpec((None, LANES, D), tok),
            scratch_shapes=[pltpu.VMEM((N_EXPERTS, GATHER_WIN, D), bf16),
                            pltpu.SemaphoreType.DMA((N_EXPERTS,))]),
        out_shape=jax.ShapeDtypeStruct((B, S, D), f32),
        compiler_params=_cp(("parallel", "arbitrary")),
        name="moe_scatter",
    )(base_flat, x3, pos3, sel3, o4)


def _pad_last(a, n):
    return jnp.pad(a, [(0, 0)] * (a.ndim - 1) + [(0, n - a.shape[-1])])


def _rope_tables(S):
    pos = jnp.arange(S, dtype=f32)
    inv_freq = ROPE_THETA ** (-jnp.arange(0, QK_ROPE, 2, dtype=f32) / QK_ROPE)
    ang = pos[:, None] * inv_freq[None, :]
    cos, sin = jnp.cos(ang), jnp.sin(ang)
    half = QK_ROPE // 2
    z_lo = jnp.zeros((S, QK_NOPE), f32)
    z_hi = jnp.zeros((S, LANES - QK_DIM), f32)
    z_half = jnp.zeros((S, half), f32)
    ct = jnp.concatenate([jnp.ones((S, QK_NOPE), f32), cos, cos, z_hi], axis=1)
    s1 = jnp.concatenate([z_lo, -sin, z_half, z_hi], axis=1)
    s2 = jnp.concatenate([z_lo, z_half, sin, z_hi], axis=1)
    return ct, s1, s2


def _hyena_positions(L):
    pos = jnp.arange(L, dtype=f32)
    t = pos / L
    bands = (FILT_EMB - 1) // 2
    f = jnp.linspace(1e-4, bands - 1, bands, dtype=f32)
    w = 2.0 * math.pi * t[:, None] * f[None, :]
    feats = jnp.concatenate([t[:, None], jnp.cos(w), jnp.sin(w)], axis=-1)
    tau = jnp.abs(pos - (L // 2)) / max(L // 2, 1)
    max_decay = math.log(DECAY_TARGET) / FAST_DECAY
    min_decay = math.log(DECAY_TARGET) / SLOW_DECAY
    deltas = jnp.abs(jnp.linspace(min_decay, max_decay, HY_W, dtype=f32))
    planes = lambda t: t.reshape(L // DFT_MINOR, DFT_MINOR, -1).transpose(1, 0, 2).reshape(L, -1)
    return planes(_pad_last(feats, LANES)), planes(tau[:, None]), jnp.tile(deltas, HY_ORDER)[None, :]


def _prep_weights(p):
    L = p["w_in"].shape[0]
    D = p["w_in"].shape[1]
    w_in = p["w_in"]
    kr = jnp.zeros((L, D, LANES), f32).at[:, :, QK_NOPE:QK_DIM].set(w_in[:, :, COL_KROPE:COL_HYENA])
    wq = _pad_last(p["w_q_b"].reshape(L, Q_LORA, N_HEADS, QK_DIM), LANES).reshape(L, Q_LORA, N_HEADS * LANES)
    wkv = p["w_kv_b"].reshape(L, KV_LORA, N_HEADS, QK_NOPE + V_DIM)
    wk = _pad_last(wkv[..., :QK_NOPE], LANES).reshape(L, KV_LORA, N_HEADS * LANES)
    wv = _pad_last(wkv[..., QK_NOPE:], LANES).reshape(L, KV_LORA, N_HEADS * LANES)
    padh = LANES - FILT_HID
    return dict(
        norm1_w=p["norm1_w"][:, None, :],
        wa=jnp.concatenate([w_in[:, :, COL_Q:COL_KROPE], kr], axis=-1).astype(bf16),
        wu=w_in[:, :, COL_HYENA:COL_GATE].astype(bf16),
        wg=w_in[:, :, COL_GATE:].astype(bf16),
        qaw=p["q_a_norm_w"][:, None, :], kvaw=p["kv_a_norm_w"][:, None, :],
        wq=wq.astype(bf16), wk=wk.astype(bf16), wv=wv.astype(bf16),
        qnw=_pad_last(p["q_norm_w"], LANES)[:, None, :], knw=_pad_last(p["k_norm_w"], LANES)[:, None, :],
        scw=p["short_conv_w"], scb=p["short_conv_b"][:, None, :],
        fw1=jnp.pad(p["filt_w1"], ((0, 0), (0, LANES - FILT_EMB), (0, padh))),
        fb1=_pad_last(p["filt_b1"], LANES)[:, None, :], ff1=_pad_last(p["filt_freq1"], LANES)[:, None, :],
        fw2=jnp.pad(p["filt_w2"], ((0, 0), (0, padh), (0, padh))),
        fb2=_pad_last(p["filt_b2"], LANES)[:, None, :], ff2=_pad_last(p["filt_freq2"], LANES)[:, None, :],
        fw3=jnp.pad(p["filt_w3"], ((0, 0), (0, padh), (0, 0))),
        skip=p["filt_skip"],
        w_attn=p["w_attn_branch"].astype(bf16), w_hy=p["w_hyena_branch"].astype(bf16),
        w_out=p["w_out"].astype(bf16),
        norm2_w=p["norm2_w"][:, None, :], wr=_pad_last(p["router_w"], LANES),
        ewg=p["expert_w_gate"].astype(bf16), ewu=p["expert_w_up"].astype(bf16),
        ewd=p["expert_w_down"].astype(bf16),
    )


def _layer(x, w, consts):
    B, S, D = x.shape
    T = B * S
    E = N_EXPERTS
    cap = max(1, EC_FACTOR * S // E)
    nh = S // DFT_MINOR

    a, u, g = _in_proj(x.reshape(T, D), w["norm1_w"], w["wa"], w["wu"], w["wg"])

    q, k, v = _mla_prep(a.reshape(B, S, -1), w["qaw"], w["kvaw"], w["wq"], w["wk"], w["wv"],
                        w["qnw"], w["knw"], consts["ct"], consts["s1"], consts["s2"])
    attn = _flash(q, k, v)
    attn = attn.transpose(0, 2, 1, 3).reshape(T, N_HEADS * V_DIM)

    ut = u.reshape(B, nh, DFT_MINOR, -1).transpose(0, 2, 1, 3)
    uct = _short_conv(ut, w["scw"], w["scb"])
    h_un, l1 = _filters(consts["feats"], consts["tau"], w["fw1"], w["fb1"], w["ff1"], w["fw2"], w["fb2"],
                        w["ff2"], w["fw3"], consts["deltas"])
    tb = consts["dft"]
    ha = _dft_a(h_un.reshape(DFT_MINOR, nh, HY_ORDER * HY_W), 0, HY_ORDER * HY_W,
                tb["m1"], tb["twc"], tb["tws"], packed=False)
    hspec = _dft_b_filter(ha, tb["m2f"], 1.0 / l1)
    z4, z_blk = uct, 0
    for n in range(HY_ORDER):
        za = _dft_a(z4, z_blk, HY_W, tb["m1"], tb["twc"], tb["tws"], packed=True)
        zb = _dft_b(za, hspec, n, tb["m2f"], tb["m2i"], tb["twc_t"], tb["tws_t"])
        z4 = _dft_c(zb, tb["gm"], z4, z_blk, uct, n + 1, w["skip"][n:n + 1])
        z_blk = 0
    hy = z4.transpose(0, 2, 1, 3).reshape(T, HY_W)

    x2, hn, aff = _merge(x.reshape(T, D), attn, hy, g, w["w_attn"], w["w_hy"], w["w_out"], w["norm2_w"], w["wr"])

    aff3 = aff.reshape(B, S, LANES)
    aff_t = aff3[:, :, :E].transpose(0, 2, 1)
    thr, need = _thresholds(aff_t.reshape(B * E, S), cap)
    pad_inf = jnp.full((B, LANES - E), jnp.inf, f32)
    thr3 = jnp.concatenate([thr[:, 0].reshape(B, E), pad_inf], axis=1)[:, None, :]
    need3 = _pad_last(need[:, 0].reshape(B, E), LANES)[:, None, :]
    sel, pos, base = _route(aff3, thr3, need3)
    base_flat = base[:, :, :E].transpose(0, 2, 1).reshape(-1).astype(jnp.int32)
    to_rows = lambda t: t[:, :, :E].transpose(0, 2, 1)[:, :, None, :]
    xs, gs = _gather(base_flat, hn.reshape(B, S, D), to_rows(pos), to_rows(sel), aff_t[:, :, None, :], cap)
    o = _ffn(xs, gs, w["ewg"], w["ewu"], w["ewd"])
    return _scatter(base_flat, x2.reshape(B, S, D), pos, sel, o)


def kernel(x, norm1_w, w_in, q_a_norm_w, w_q_b, kv_a_norm_w, w_kv_b, q_norm_w, k_norm_w, short_conv_w,
           short_conv_b, filt_w1, filt_b1, filt_freq1, filt_w2, filt_b2, filt_freq2, filt_w3, filt_skip,
           w_attn_branch, w_hyena_branch, w_out, norm2_w, router_w, expert_w_gate, expert_w_up, expert_w_down):
    B, S, D = x.shape
    assert S % (2 * DFT_MINOR) == 0 and w_hyena_branch.shape[1] == HY_W
    params = dict(norm1_w=norm1_w, w_in=w_in, q_a_norm_w=q_a_norm_w, w_q_b=w_q_b, kv_a_norm_w=kv_a_norm_w,
                  w_kv_b=w_kv_b, q_norm_w=q_norm_w, k_norm_w=k_norm_w, short_conv_w=short_conv_w,
                  short_conv_b=short_conv_b, filt_w1=filt_w1, filt_b1=filt_b1, filt_freq1=filt_freq1,
                  filt_w2=filt_w2, filt_b2=filt_b2, filt_freq2=filt_freq2, filt_w3=filt_w3, filt_skip=filt_skip,
                  w_attn_branch=w_attn_branch, w_hyena_branch=w_hyena_branch, w_out=w_out, norm2_w=norm2_w,
                  router_w=router_w, expert_w_gate=expert_w_gate, expert_w_up=expert_w_up,
                  expert_w_down=expert_w_down)
    weights = _prep_weights(params)
    ct, s1, s2 = _rope_tables(S)
    feats, tau, deltas = _hyena_positions(S)
    consts = dict(ct=ct, s1=s1, s2=s2, feats=feats, tau=tau, deltas=deltas, dft=_dft_tables(S))

    def body(xc, w):
        return _layer(xc, w, consts), None

    out, _ = lax.scan(body, x, weights)
    return out
```

```python
import functools
import math

import jax
import jax.numpy as jnp
from jax import lax
from jax.experimental import pallas as pl
from jax.experimental.pallas import tpu as pltpu

f32 = jnp.float32
bf16 = jnp.bfloat16

N_HEADS = 16
QK_NOPE = 64
QK_ROPE = 32
QK_DIM = QK_NOPE + QK_ROPE
V_DIM = 64
Q_LORA = 384
KV_LORA = 256
ROPE_THETA = 10000.0
HY_W = 1024
HY_ORDER = 2
FILT_EMB = 33
FILT_HID = 64
DECAY_TARGET = 1e-2
FAST_DECAY = 0.3
SLOW_DECAY = 1.5
N_EXPERTS = 16
EC_FACTOR = 2
RMS_EPS = 1e-6

COL_Q = 0
COL_KV = COL_Q + Q_LORA
COL_KROPE = COL_KV + KV_LORA
COL_HYENA = COL_KROPE + QK_ROPE
COL_GATE = COL_HYENA + (HY_ORDER + 1) * HY_W

LANES = 128
SUBLANES = 8
DFT_MINOR = 128
GATHER_WIN = 144
VMEM_LIMIT = 56 * 1024 * 1024


def _cp(sem, vmem=VMEM_LIMIT):
    return pltpu.CompilerParams(dimension_semantics=sem, vmem_limit_bytes=vmem)


def _tile(n, pref):
    t = min(n, pref)
    assert n % t == 0, (n, pref)
    return t


def _dot(a, b):
    return jnp.dot(a, b, preferred_element_type=f32)


def _dot_hi(a, b):
    return jnp.dot(a, b, preferred_element_type=f32, precision=lax.Precision.HIGHEST)


def _in_proj_kernel(x_ref, nw_ref, wa_ref, wu_ref, wg_ref, a_ref, u_ref, g_ref):
    x = x_ref[...]
    ms = jnp.mean(x * x, axis=-1, keepdims=True)
    xn = (x * lax.rsqrt(ms + RMS_EPS) * nw_ref[...]).astype(bf16)
    a_ref[...] = _dot(xn, wa_ref[...])
    u_ref[...] = _dot(xn, wu_ref[...]).astype(bf16)
    g_ref[...] = jax.nn.sigmoid(_dot(xn, wg_ref[...])).astype(bf16)


def _in_proj(x2, nw, wa, wu, wg):
    T, D = x2.shape
    tm = _tile(T, 256)
    na, nu, ng = wa.shape[1], wu.shape[1], wg.shape[1]
    full = lambda i: (0, 0)
    row = lambda i: (i, 0)
    return pl.pallas_call(
        _in_proj_kernel,
        grid=(T // tm,),
        in_specs=[pl.BlockSpec((tm, D), row), pl.BlockSpec((1, D), full),
                  pl.BlockSpec((D, na), full), pl.BlockSpec((D, nu), full), pl.BlockSpec((D, ng), full)],
        out_specs=[pl.BlockSpec((tm, na), row), pl.BlockSpec((tm, nu), row), pl.BlockSpec((tm, ng), row)],
        out_shape=[jax.ShapeDtypeStruct((T, na), f32), jax.ShapeDtypeStruct((T, nu), bf16),
                   jax.ShapeDtypeStruct((T, ng), bf16)],
        compiler_params=_cp(("parallel",)),
        name="in_proj",
    )(x2, nw, wa, wu, wg)


def _head_norm_rope(t, w, ct, s1, s2):
    ss = jnp.sum(t * t, axis=-1, keepdims=True) * (1.0 / QK_DIM)
    t = t * lax.rsqrt(ss + RMS_EPS) * w
    return t * ct + pltpu.roll(t, LANES - QK_ROPE // 2, 1) * s1 + pltpu.roll(t, QK_ROPE // 2, 1) * s2


def _mla_prep_kernel(a_ref, qaw_ref, kvaw_ref, wq_ref, wk_ref, wv_ref, qnw_ref, knw_ref,
                     ct_ref, s1_ref, s2_ref, q_ref, k_ref, v_ref, *, qscale):
    a = a_ref[...]
    ql = a[:, :Q_LORA]
    kvl = a[:, Q_LORA:Q_LORA + KV_LORA]
    kr = a[:, Q_LORA + KV_LORA:]
    qn = (ql * lax.rsqrt(jnp.mean(ql * ql, axis=-1, keepdims=True) + RMS_EPS) * qaw_ref[...]).astype(bf16)
    kvn = (kvl * lax.rsqrt(jnp.mean(kvl * kvl, axis=-1, keepdims=True) + RMS_EPS) * kvaw_ref[...]).astype(bf16)
    q = _dot(qn, wq_ref[...])
    k = _dot(kvn, wk_ref[...])
    v = _dot(kvn, wv_ref[...])
    ct, s1, s2 = ct_ref[...], s1_ref[...], s2_ref[...]
    lane = lax.broadcasted_iota(jnp.int32, (1, LANES), 1)
    ones_col = (lane == V_DIM).astype(f32)
    for h in range(N_HEADS):
        sl = slice(h * LANES, (h + 1) * LANES)
        qh = _head_norm_rope(q[:, sl], qnw_ref[...], ct, s1, s2) * qscale
        kh = _head_norm_rope(k[:, sl] + kr, knw_ref[...], ct, s1, s2)
        q_ref[h] = qh.astype(bf16)
        k_ref[h] = kh.astype(bf16)
        v_ref[h] = (v[:, sl] + ones_col).T.astype(bf16)


def _mla_prep(a3, qaw, kvaw, wq, wk, wv, qnw, knw, ct, s1, s2):
    B, S, NA = a3.shape
    tm = _tile(S, 512)
    HP = N_HEADS * LANES
    qscale = (QK_DIM ** -0.5) * math.log2(math.e)
    full2 = lambda b, i: (0, 0)
    pos = lambda b, i: (i, 0)
    out_spec = pl.BlockSpec((None, N_HEADS, tm, LANES), lambda b, i: (b, 0, i, 0))
    out_sds = jax.ShapeDtypeStruct((B, N_HEADS, S, LANES), bf16)
    return pl.pallas_call(
        functools.partial(_mla_prep_kernel, qscale=qscale),
        grid=(B, S // tm),
        in_specs=[pl.BlockSpec((None, tm, NA), lambda b, i: (b, i, 0)),
                  pl.BlockSpec((1, Q_LORA), full2), pl.BlockSpec((1, KV_LORA), full2),
                  pl.BlockSpec((Q_LORA, HP), full2), pl.BlockSpec((KV_LORA, HP), full2),
                  pl.BlockSpec((KV_LORA, HP), full2),
                  pl.BlockSpec((1, LANES), full2), pl.BlockSpec((1, LANES), full2),
                  pl.BlockSpec((tm, LANES), pos), pl.BlockSpec((tm, LANES), pos), pl.BlockSpec((tm, LANES), pos)],
        out_specs=[out_spec, out_spec,
                   pl.BlockSpec((None, N_HEADS, None, LANES, tm), lambda b, i: (b, 0, i, 0, 0))],
        out_shape=[out_sds, out_sds, jax.ShapeDtypeStruct((B, N_HEADS, S // tm, LANES, tm), bf16)],
        compiler_params=_cp(("parallel", "parallel")),
        name="mla_prep",
    )(a3, qaw, kvaw, wq, wk, wv, qnw, knw, ct, s1, s2)


def _flash_kernel(q_ref, k_ref, vt_ref, o_ref, m_sc, acc_sc, s0_sc, s1_sc, *, unroll):
    nchunk, _, ck = vt_ref.shape
    q = q_ref[...]
    m_sc[...] = jnp.full(m_sc.shape, -jnp.inf, f32)
    acc_sc[...] = jnp.zeros(acc_sc.shape, f32)
    sbuf = (s0_sc, s1_sc)

    def scores(c, buf):
        start = pl.multiple_of(c * ck, ck)
        buf[...] = lax.dot_general(k_ref[pl.ds(start, ck), :], q, (((1,), (1,)), ((), ())),
                                   preferred_element_type=f32)

    def softmax_pv(c, buf):
        st = buf[...]
        m = m_sc[...]
        m_new = jnp.maximum(m, jnp.max(st, axis=0, keepdims=True))
        pt = jnp.exp2(st - m_new).astype(bf16)
        acc_sc[...] = jnp.exp2(m - m_new) * acc_sc[...] + _dot(vt_ref[c], pt)
        m_sc[...] = m_new

    scores(0, s0_sc)

    def body(i, carry):
        for u in range(unroll):
            c = i * unroll + u
            scores(jnp.minimum(c + 1, nchunk - 1), sbuf[(u + 1) % 2])
            softmax_pv(c, sbuf[u % 2])
        return carry

    lax.fori_loop(0, nchunk // unroll, body, 0)
    acc = acc_sc[...].T
    o_ref[...] = (acc[:, :V_DIM] / acc[:, V_DIM:V_DIM + 1]).astype(o_ref.dtype)


def _flash(q, k, vt):
    B, H, S, _ = q.shape
    nchunk, ck = vt.shape[2], vt.shape[4]
    tq = _tile(S, 512)
    unroll = 4 if nchunk % 4 == 0 else 2
    assert nchunk % unroll == 0
    return pl.pallas_call(
        functools.partial(_flash_kernel, unroll=unroll),
        grid=(B, H, S // tq),
        in_specs=[pl.BlockSpec((None, None, tq, LANES), lambda b, h, i: (b, h, i, 0)),
                  pl.BlockSpec((None, None, S, LANES), lambda b, h, i: (b, h, 0, 0)),
                  pl.BlockSpec((None, None, nchunk, LANES, ck), lambda b, h, i: (b, h, 0, 0, 0))],
        out_specs=pl.BlockSpec((None, None, tq, V_DIM), lambda b, h, i: (b, h, i, 0)),
        out_shape=jax.ShapeDtypeStruct((B, H, S, V_DIM), bf16),
        scratch_shapes=[pltpu.VMEM((1, tq), f32), pltpu.VMEM((LANES, tq), f32),
                        pltpu.VMEM((ck, tq), f32), pltpu.VMEM((ck, tq), f32)],
        compiler_params=_cp(("parallel", "parallel", "parallel")),
        name="flash_attn",
    )(q, k, vt)


def _short_conv_kernel(u_ref, prev_ref, next_ref, w_ref, b_ref, o_ref):
    j = pl.program_id(1)
    last = pl.num_programs(1) - 1
    cur = u_ref[...].astype(f32)
    prev = prev_ref[...].astype(f32)
    nxt = next_ref[...].astype(f32)
    nh = cur.shape[0]
    rows = lax.broadcasted_iota(jnp.int32, (nh, 1), 0)
    prev_wrap = jnp.where(rows == 0, 0.0, pltpu.roll(prev, 1, 0))
    next_wrap = jnp.where(rows == nh - 1, 0.0, pltpu.roll(nxt, nh - 1, 0))
    up = jnp.where(j == 0, prev_wrap, prev)
    dn = jnp.where(j == last, next_wrap, nxt)
    w = w_ref[...]
    o_ref[...] = (b_ref[...] + up * w[0:1, :] + cur * w[1:2, :] + dn * w[2:3, :]).astype(o_ref.dtype)


def _short_conv(ut, w, b):
    B, P, nh, CT = ut.shape
    tc = _tile(CT, 1024)
    plane = lambda off: pl.BlockSpec((None, None, nh, tc), lambda bi, j, c: (bi, (j + off + P) % P, 0, c))
    return pl.pallas_call(
        _short_conv_kernel,
        grid=(B, P, CT // tc),
        in_specs=[plane(0), plane(-1), plane(1),
                  pl.BlockSpec((3, tc), lambda bi, j, c: (0, c)),
                  pl.BlockSpec((1, tc), lambda bi, j, c: (0, c))],
        out_specs=plane(0),
        out_shape=jax.ShapeDtypeStruct((B, P, nh, CT), bf16),
        compiler_params=_cp(("parallel", "parallel", "parallel")),
        name="short_conv",
    )(ut, ut, ut, w, b)


def _filter_kernel(feat_ref, tau_ref, w1_ref, b1_ref, f1_ref, w2_ref, b2_ref, f2_ref, w3_ref, dl_ref,
                   h_ref, l1_ref):
    i = pl.program_id(0)
    h = jnp.sin(f1_ref[...] * (_dot_hi(feat_ref[...], w1_ref[...]) + b1_ref[...]))
    h = jnp.sin(f2_ref[...] * (_dot_hi(h, w2_ref[...]) + b2_ref[...]))
    h = _dot_hi(h, w3_ref[...])
    h = h * jnp.exp(-tau_ref[...] * dl_ref[...])
    h_ref[...] = h

    @pl.when(i == 0)
    def _():
        l1_ref[...] = jnp.zeros(l1_ref.shape, f32)

    l1_ref[...] += jnp.sum(jnp.abs(h), axis=0, keepdims=True)


def _filters(feats, tau, w1, b1, f1, w2, b2, f2, w3, dl):
    S = feats.shape[0]
    CO = w3.shape[1]
    tm = _tile(S, 512)
    full = lambda i: (0, 0)
    row = lambda i: (i, 0)
    return pl.pallas_call(
        _filter_kernel,
        grid=(S // tm,),
        in_specs=[pl.BlockSpec((tm, LANES), row), pl.BlockSpec((tm, 1), row),
                  pl.BlockSpec((LANES, LANES), full), pl.BlockSpec((1, LANES), full), pl.BlockSpec((1, LANES), full),
                  pl.BlockSpec((LANES, LANES), full), pl.BlockSpec((1, LANES), full), pl.BlockSpec((1, LANES), full),
                  pl.BlockSpec((LANES, CO), full), pl.BlockSpec((1, CO), full)],
        out_specs=[pl.BlockSpec((tm, CO), row), pl.BlockSpec((1, CO), full)],
        out_shape=[jax.ShapeDtypeStruct((S, CO), f32), jax.ShapeDtypeStruct((1, CO), f32)],
        compiler_params=_cp(("arbitrary",)),
        name="hyena_filters",
    )(feats, tau, w1, b1, f1, w2, b2, f2, w3, dl)


def _dft_a_kernel(x_ref, m_ref, tc_ref, ts_ref, a_ref, *, packed):
    n1f = m_ref.shape[0] // 2
    m = m_ref[...]
    if packed:
        pr = _dot(m, x_ref[0].astype(bf16))
        pi = _dot(m, x_ref[1].astype(bf16))
        ar = pr[:n1f] + pi[n1f:]
        ai = pi[:n1f] - pr[n1f:]
    else:
        p = _dot(m, x_ref[...].astype(bf16))
        ar = p[:n1f]
        ai = -p[n1f:]
    tc, ts = tc_ref[...], ts_ref[...]
    a_ref[0] = (ar * tc + ai * ts).astype(a_ref.dtype)
    a_ref[1] = (ai * tc - ar * ts).astype(a_ref.dtype)


def _dft_a(x4, col_blk, C, m1, twc, tws, packed):
    n1f = m1.shape[0] // 2
    nh = m1.shape[1]
    if packed:
        x_spec = pl.BlockSpec((2, None, nh, C), lambda j: (0, j, 0, col_blk))
    else:
        x_spec = pl.BlockSpec((None, nh, C), lambda j: (j, 0, col_blk))
    return pl.pallas_call(
        functools.partial(_dft_a_kernel, packed=packed),
        grid=(DFT_MINOR,),
        in_specs=[x_spec,
                  pl.BlockSpec((2 * n1f, nh), lambda j: (0, 0)),
                  pl.BlockSpec((None, n1f, 1), lambda j: (j, 0, 0)),
                  pl.BlockSpec((None, n1f, 1), lambda j: (j, 0, 0))],
        out_specs=pl.BlockSpec((2, None, n1f, C), lambda j: (0, j, 0, 0)),
        out_shape=jax.ShapeDtypeStruct((2, DFT_MINOR, n1f, C), bf16),
        compiler_params=_cp(("parallel",)),
        name="dft_stage_a",
    )(x4, m1, twc, tws)


def _dft_bf_kernel(a_ref, m2_ref, inv_ref, h_ref):
    C = a_ref.shape[-1]
    a = a_ref[...].reshape(2 * DFT_MINOR, C)
    z = _dot(m2_ref[...], a) * inv_ref[...]
    h_ref[...] = z.reshape(2, DFT_MINOR, C).astype(h_ref.dtype)


def _dft_b_filter(a4, m2f, inv_l1):
    _, _, n1f, C = a4.shape
    a4 = a4.swapaxes(1, 2)
    return pl.pallas_call(
        _dft_bf_kernel,
        grid=(n1f,),
        in_specs=[pl.BlockSpec((2, None, DFT_MINOR, C), lambda k: (0, k, 0, 0)),
                  pl.BlockSpec((2 * DFT_MINOR, 2 * DFT_MINOR), lambda k: (0, 0)),
                  pl.BlockSpec((1, C), lambda k: (0, 0))],
        out_specs=pl.BlockSpec((None, 2, DFT_MINOR, C), lambda k: (k, 0, 0, 0)),
        out_shape=jax.ShapeDtypeStruct((n1f, 2, DFT_MINOR, C), bf16),
        compiler_params=_cp(("parallel",)),
        name="dft_stage_b_filter",
    )(a4, m2f, inv_l1)


def _dft_b_kernel(a_ref, h_ref, m2f_ref, m2i_ref, tc_ref, ts_ref, b_ref):
    C = a_ref.shape[-1]
    a = a_ref[...].reshape(2 * DFT_MINOR, C)
    z = _dot(m2f_ref[...], a)
    zr, zi = z[:DFT_MINOR], z[DFT_MINOR:]
    hr, hi = h_ref[0].astype(f32), h_ref[1].astype(f32)
    y = jnp.concatenate([zr * hr - zi * hi, zr * hi + zi * hr], axis=0).astype(bf16)
    w = _dot(m2i_ref[...], y)
    wr, wi = w[:DFT_MINOR], w[DFT_MINOR:]
    tc, ts = tc_ref[...], ts_ref[...]
    b_ref[0] = (wr * tc - wi * ts).astype(b_ref.dtype)
    b_ref[1] = (wr * ts + wi * tc).astype(b_ref.dtype)


def _dft_b(a4, hspec, order, m2f, m2i, twc_t, tws_t):
    _, _, n1f, C = a4.shape
    a4 = a4.swapaxes(1, 2)
    return pl.pallas_call(
        _dft_b_kernel,
        grid=(n1f,),
        in_specs=[pl.BlockSpec((2, None, DFT_MINOR, C), lambda k: (0, k, 0, 0)),
                  pl.BlockSpec((None, 2, DFT_MINOR, C), lambda k: (k, 0, 0, order)),
                  pl.BlockSpec((2 * DFT_MINOR, 2 * DFT_MINOR), lambda k: (0, 0)),
                  pl.BlockSpec((2 * DFT_MINOR, 2 * DFT_MINOR), lambda k: (0, 0)),
                  pl.BlockSpec((None, DFT_MINOR, 1), lambda k: (k, 0, 0)),
                  pl.BlockSpec((None, DFT_MINOR, 1), lambda k: (k, 0, 0))],
        out_specs=pl.BlockSpec((2, None, DFT_MINOR, C), lambda k: (0, k, 0, 0)),
        out_shape=jax.ShapeDtypeStruct((2, n1f, DFT_MINOR, C), bf16),
        compiler_params=_cp(("parallel",)),
        name="dft_stage_b",
    )(a4, hspec, m2f, m2i, twc_t, tws_t)


def _dft_c_kernel(b_ref, g_ref, z_ref, xg_ref, skip_ref, o_ref):
    n1f = b_ref.shape[1]
    C = b_ref.shape[-1]
    nh = n1f // 2
    bb = b_ref[...].reshape(2 * n1f, C)
    y = _dot(g_ref[...], bb)
    skip = skip_ref[...]
    for bi in range(2):
        z = z_ref[bi].astype(f32)
        o_ref[bi] = (xg_ref[bi].astype(f32) * (y[bi * nh:(bi + 1) * nh] + z * skip)).astype(o_ref.dtype)


def _dft_c(b4, gm, z4, z_blk, xg4, xg_blk, skip):
    _, n1f, _, C = b4.shape
    nh = n1f // 2
    b4 = b4.swapaxes(1, 2)
    return pl.pallas_call(
        _dft_c_kernel,
        grid=(DFT_MINOR,),
        in_specs=[pl.BlockSpec((2, None, n1f, C), lambda j: (0, j, 0, 0)),
                  pl.BlockSpec((n1f, 2 * n1f), lambda j: (0, 0)),
                  pl.BlockSpec((2, None, nh, C), lambda j: (0, j, 0, z_blk)),
                  pl.BlockSpec((2, None, nh, C), lambda j: (0, j, 0, xg_blk)),
                  pl.BlockSpec((1, C), lambda j: (0, 0))],
        out_specs=pl.BlockSpec((2, None, nh, C), lambda j: (0, j, 0, 0)),
        out_shape=jax.ShapeDtypeStruct((2, DFT_MINOR, nh, C), bf16),
        compiler_params=_cp(("parallel",)),
        name="dft_stage_c",
    )(b4, gm, z4, xg4, skip)


def _dft_tables(S):
    N = 2 * S
    n1f = N // DFT_MINOR
    nh = n1f // 2
    two_pi = 2.0 * math.pi

    def cs(num, den):
        ang = (num % den).astype(f32) * (two_pi / den)
        return jnp.cos(ang), jnp.sin(ang)

    k1 = jnp.arange(n1f, dtype=jnp.int32)
    n1 = jnp.arange(nh, dtype=jnp.int32)
    c1, s1 = cs(k1[:, None] * n1[None, :], n1f)
    m1 = jnp.concatenate([c1, s1], axis=0).astype(bf16)
    n2 = jnp.arange(DFT_MINOR, dtype=jnp.int32)
    twc, tws = cs(n2[:, None] * k1[None, :], N)
    c2, s2 = cs(n2[:, None] * n2[None, :], DFT_MINOR)
    m2f = jnp.concatenate([jnp.concatenate([c2, s2], 1), jnp.concatenate([-s2, c2], 1)], 0).astype(bf16)
    m2i = jnp.concatenate([jnp.concatenate([c2, -s2], 1), jnp.concatenate([s2, c2], 1)], 0).astype(bf16)
    rows = jnp.arange(nh, dtype=jnp.int32) + n1f // 4
    gc, gs = cs(rows[:, None] * k1[None, :], n1f)
    gm = (jnp.concatenate([jnp.concatenate([gc, -gs], 1), jnp.concatenate([gs, gc], 1)], 0) / N).astype(bf16)
    return dict(m1=m1, twc=twc[:, :, None], tws=tws[:, :, None],
                twc_t=twc.T[:, :, None], tws_t=tws.T[:, :, None], m2f=m2f, m2i=m2i, gm=gm)


def _merge_kernel(x_ref, at_ref, hy_ref, g_ref, wa_ref, wh_ref, wo_ref, n2_ref, wr_ref,
                  xo_ref, hn_ref, aff_ref):
    D = x_ref.shape[1]
    g = g_ref[...].astype(f32)
    merged = g[:, :D] * _dot(at_ref[...], wa_ref[...]) + g[:, D:] * _dot(hy_ref[...], wh_ref[...])
    x = x_ref[...] + _dot(merged.astype(bf16), wo_ref[...])
    xo_ref[...] = x
    hn = x * lax.rsqrt(jnp.mean(x * x, axis=-1, keepdims=True) + RMS_EPS) * n2_ref[...]
    hn_ref[...] = hn.astype(bf16)
    logits = _dot_hi(hn, wr_ref[...])
    lane = lax.broadcasted_iota(jnp.int32, logits.shape, 1)
    logits = jnp.where(lane < N_EXPERTS, logits, -jnp.inf)
    e = jnp.exp(logits - jnp.max(logits, axis=-1, keepdims=True))
    aff_ref[...] = e / jnp.sum(e, axis=-1, keepdims=True)


def _merge(x2, attn, hy, g, wa, wh, wo, n2w, wr):
    T, D = x2.shape
    tm = _tile(T, 256)
    row = lambda i: (i, 0)
    full = lambda i: (0, 0)
    return pl.pallas_call(
        _merge_kernel,
        grid=(T // tm,),
        in_specs=[pl.BlockSpec((tm, D), row), pl.BlockSpec((tm, D), row), pl.BlockSpec((tm, D), row),
                  pl.BlockSpec((tm, 2 * D), row),
                  pl.BlockSpec((D, D), full), pl.BlockSpec((D, D), full), pl.BlockSpec((D, D), full),
                  pl.BlockSpec((1, D), full), pl.BlockSpec((D, LANES), full)],
        out_specs=[pl.BlockSpec((tm, D), row), pl.BlockSpec((tm, D), row), pl.BlockSpec((tm, LANES), row)],
        out_shape=[jax.ShapeDtypeStruct((T, D), f32), jax.ShapeDtypeStruct((T, D), bf16),
                   jax.ShapeDtypeStruct((T, LANES), f32)],
        compiler_params=_cp(("parallel",)),
        name="merge_out_router",
    )(x2, attn, hy, g, wa, wh, wo, n2w, wr)


def _thresh_kernel(a_ref, thr_ref, need_ref, *, cap):
    bits = pltpu.bitcast(a_ref[...], jnp.int32)
    R = bits.shape[0]

    def body(i, thr):
        cand = thr | jnp.left_shift(jnp.int32(1), 30 - i)
        cnt = jnp.sum((bits >= cand).astype(f32), axis=1, keepdims=True)
        return jnp.where(cnt >= cap, cand, thr)

    thr = lax.fori_loop(0, 31, body, jnp.zeros((R, 1), jnp.int32))
    n_gt = jnp.sum((bits > thr).astype(f32), axis=1, keepdims=True)
    thr_ref[...] = jnp.broadcast_to(pltpu.bitcast(thr, f32), thr_ref.shape)
    need_ref[...] = jnp.broadcast_to(cap - n_gt, need_ref.shape)


def _thresholds(aff_t, cap):
    R, S = aff_t.shape
    return pl.pallas_call(
        functools.partial(_thresh_kernel, cap=float(cap)),
        out_shape=[jax.ShapeDtypeStruct((R, LANES), f32), jax.ShapeDtypeStruct((R, LANES), f32)],
        compiler_params=_cp(None),
        name="route_threshold",
    )(aff_t)


def _route_kernel(aff_ref, thr_ref, need_ref, ltri_ref, slab_ref, sel_ref, pos_ref, base_ref, ceq_sc, csel_sc):
    k = pl.program_id(1)

    @pl.when(k == 0)
    def _():
        ceq_sc[...] = jnp.zeros(ceq_sc.shape, f32)
        csel_sc[...] = jnp.zeros(csel_sc.shape, f32)

    a = aff_ref[...]
    thr = thr_ref[...]
    gt = a > thr
    eq = a == thr
    eqf = eq.astype(f32)
    ltri = ltri_ref[...]
    eq_rank = ceq_sc[...] + _dot(ltri, eqf.astype(bf16))
    sel = jnp.logical_or(gt, jnp.logical_and(eq, eq_rank < need_ref[...]))
    self_ = sel.astype(f32)
    pos = csel_sc[...] + _dot(ltri, self_.astype(bf16))
    sel_ref[...] = self_
    pos_ref[...] = pos
    base_ref[...] = csel_sc[...] + _dot(slab_ref[...], self_.astype(bf16))
    ceq_sc[...] += jnp.sum(eqf, axis=0, keepdims=True)
    csel_sc[...] += jnp.sum(self_, axis=0, keepdims=True)


def _route(aff3, thr3, need3):
    B, S, _ = aff3.shape
    tk = _tile(S, 1024)
    ltri = (jnp.arange(tk)[:, None] > jnp.arange(tk)[None, :]).astype(bf16)
    tok = pl.BlockSpec((None, tk, LANES), lambda b, k: (b, k, 0))
    per_b = pl.BlockSpec((None, 1, LANES), lambda b, k: (b, 0, 0))
    return pl.pallas_call(
        _route_kernel,
        grid=(B, S // tk),
        in_specs=[tok, per_b, per_b, pl.BlockSpec((tk, tk), lambda b, k: (0, 0)),
                  pl.BlockSpec((tk // LANES, tk), lambda b, k: (0, 0))],
        out_specs=[tok, tok, pl.BlockSpec((None, tk // LANES, LANES), lambda b, k: (b, k, 0))],
        out_shape=[jax.ShapeDtypeStruct((B, S, LANES), f32), jax.ShapeDtypeStruct((B, S, LANES), f32),
                   jax.ShapeDtypeStruct((B, S // LANES, LANES), f32)],
        scratch_shapes=[pltpu.VMEM((1, LANES), f32), pltpu.VMEM((1, LANES), f32)],
        compiler_params=_cp(("parallel", "arbitrary")),
        name="route_positions",
    )(aff3, thr3, need3, ltri, ltri[::LANES])


def _gather_kernel(base_sm, hn_ref, pos_ref, sel_ref, gate_ref, xs_ref, gs_ref, acc_sc, gacc_sc, *, nsub, nsub_tot):
    b, e, k = pl.program_id(0), pl.program_id(1), pl.program_id(2)
    cap = xs_ref.shape[0]

    @pl.when(k == 0)
    def _():
        acc_sc[...] = jnp.zeros(acc_sc.shape, f32)
        gacc_sc[...] = jnp.zeros(gacc_sc.shape, f32)

    rows = lax.broadcasted_iota(jnp.int32, (GATHER_WIN, LANES), 0).astype(f32)
    for j in range(nsub):
        base = base_sm[(b * N_EXPERTS + e) * nsub_tot + k * nsub + j]
        start = pl.multiple_of((base // SUBLANES) * SUBLANES, SUBLANES)
        sl = slice(j * LANES, (j + 1) * LANES)
        rel = pos_ref[:, sl] - start.astype(f32)
        hit = jnp.logical_and(rows == rel, sel_ref[:, sl] > 0.0)
        oht = jnp.where(hit, 1.0, 0.0)
        acc_sc[pl.ds(start, GATHER_WIN), :] += _dot(oht.astype(bf16), hn_ref[sl, :])
        gacc_sc[pl.ds(start, GATHER_WIN), :] += jnp.sum(oht * gate_ref[:, sl], axis=1, keepdims=True)

    @pl.when(k == pl.num_programs(2) - 1)
    def _():
        xs_ref[...] = acc_sc[0:cap, :].astype(xs_ref.dtype)
        gs_ref[...] = gacc_sc[0:cap, :]


def _gather(base_flat, hn3, pos_t, sel_t, gate_t, cap):
    B, S, D = hn3.shape
    tk = _tile(S, 1024)
    nsub = tk // LANES
    row = pl.BlockSpec((None, None, 1, tk), lambda b, e, k, base: (b, e, 0, k))
    return pl.pallas_call(
        functools.partial(_gather_kernel, nsub=nsub, nsub_tot=S // LANES),
        grid_spec=pltpu.PrefetchScalarGridSpec(
            num_scalar_prefetch=1,
            grid=(B, N_EXPERTS, S // tk),
            in_specs=[pl.BlockSpec((None, tk, D), lambda b, e, k, base: (b, k, 0)), row, row, row],
            out_specs=[pl.BlockSpec((None, None, cap, D), lambda b, e, k, base: (b, e, 0, 0)),
                       pl.BlockSpec((None, None, cap, 1), lambda b, e, k, base: (b, e, 0, 0))],
            scratch_shapes=[pltpu.VMEM((cap + GATHER_WIN, D), f32), pltpu.VMEM((cap + GATHER_WIN, 1), f32)]),
        out_shape=[jax.ShapeDtypeStruct((B, N_EXPERTS, cap, D), bf16),
                   jax.ShapeDtypeStruct((B, N_EXPERTS, cap, 1), f32)],
        compiler_params=_cp(("parallel", "parallel", "arbitrary")),
        name="moe_gather",
    )(base_flat, hn3, pos_t, sel_t, gate_t)


def _ffn_kernel(xs_ref, gs_ref, wg_ref, wu_ref, wd_ref, o_ref):
    xs = xs_ref[...]
    a = _dot(xs, wg_ref[...])
    u = _dot(xs, wu_ref[...])
    hmid = (a * jax.nn.sigmoid(a) * u).astype(bf16)
    o_ref[...] = (_dot(hmid, wd_ref[...]) * gs_ref[...]).astype(o_ref.dtype)


def _ffn(xs, gs, wg, wu, wd):
    B, E, cap, D = xs.shape
    F = wg.shape[-1]
    tm = _tile(cap, 512)
    return pl.pallas_call(
        _ffn_kernel,
        grid=(E, B, cap // tm),
        in_specs=[pl.BlockSpec((None, None, tm, D), lambda e, b, i: (b, e, i, 0)),
                  pl.BlockSpec((None, None, tm, 1), lambda e, b, i: (b, e, i, 0)),
                  pl.BlockSpec((None, D, F), lambda e, b, i: (e, 0, 0)),
                  pl.BlockSpec((None, D, F), lambda e, b, i: (e, 0, 0)),
                  pl.BlockSpec((None, F, D), lambda e, b, i: (e, 0, 0))],
        out_specs=pl.BlockSpec((None, None, tm, D), lambda e, b, i: (b, e, i, 0)),
        out_shape=jax.ShapeDtypeStruct((B, E, cap, D), bf16),
        compiler_params=_cp(("parallel", "parallel", "parallel")),
        name="expert_ffn",
    )(xs, gs, wg, wu, wd)


def _scatter_kernel(base_sm, x_ref, pos_ref, sel_ref, o_hbm, out_ref, buf, sem, *, nsub_tot, cap):
    b, k = pl.program_id(0), pl.program_id(1)

    def window(e):
        base = base_sm[(b * N_EXPERTS + e) * nsub_tot + k]
        start = jnp.minimum((base // SUBLANES) * SUBLANES, cap - GATHER_WIN)
        start = pl.multiple_of(start, SUBLANES)
        return start, pltpu.make_async_copy(o_hbm.at[b, e, pl.ds(start, GATHER_WIN), :], buf.at[e], sem.at[e])

    for e in range(N_EXPERTS):
        window(e)[1].start()
    acc = x_ref[...]
    pos = pos_ref[...]
    sel = sel_ref[...]
    lanes = lax.broadcasted_iota(jnp.int32, (LANES, GATHER_WIN), 1).astype(f32)
    for e in range(N_EXPERTS):
        start, cp = window(e)
        cp.wait()
        rel = pos[:, e:e + 1] - start.astype(f32)
        hit = jnp.logical_and(lanes == rel, sel[:, e:e + 1] > 0.0)
        acc = acc + _dot(jnp.where(hit, 1.0, 0.0).astype(bf16), buf[e])
    out_ref[...] = acc


def _scatter(base_flat, x3, pos3, sel3, o4):
    B, S, D = x3.shape
    cap = o4.shape[2]
    tok = lambda b, k, base: (b, k, 0)
    return pl.pallas_call(
        functools.partial(_scatter_kernel, nsub_tot=S // LANES, cap=cap),
        grid_spec=pltpu.PrefetchScalarGridSpec(
            num_scalar_prefetch=1,
            grid=(B, S // LANES),
            in_specs=[pl.BlockSpec((None, LANES, D), tok), pl.BlockSpec((None, LANES, LANES), tok),
                      pl.BlockSpec((None, LANES, LANES), tok), pl.BlockSpec(memory_space=pl.ANY)],
            out_specs=pl.BlockSpec((None, LANES, D), tok),
            scratch_shapes=[pltpu.VMEM((N_EXPERTS, GATHER_WIN, D), bf16),
                            pltpu.SemaphoreType.DMA((N_EXPERTS,))]),
        out_shape=jax.ShapeDtypeStruct((B, S, D), f32),
        compiler_params=_cp(("parallel", "arbitrary")),
        name="moe_scatter",
    )(base_flat, x3, pos3, sel3, o4)


def _pad_last(a, n):
    return jnp.pad(a, [(0, 0)] * (a.ndim - 1) + [(0, n - a.shape[-1])])


def _rope_tables(S):
    pos = jnp.arange(S, dtype=f32)
    inv_freq = ROPE_THETA ** (-jnp.arange(0, QK_ROPE, 2, dtype=f32) / QK_ROPE)
    ang = pos[:, None] * inv_freq[None, :]
    cos, sin = jnp.cos(ang), jnp.sin(ang)
    half = QK_ROPE // 2
    z_lo = jnp.zeros((S, QK_NOPE), f32)
    z_hi = jnp.zeros((S, LANES - QK_DIM), f32)
    z_half = jnp.zeros((S, half), f32)
    ct = jnp.concatenate([jnp.ones((S, QK_NOPE), f32), cos, cos, z_hi], axis=1)
    s1 = jnp.concatenate([z_lo, -sin, z_half, z_hi], axis=1)
    s2 = jnp.concatenate([z_lo, z_half, sin, z_hi], axis=1)
    return ct, s1, s2


def _hyena_positions(L):
    pos = jnp.arange(L, dtype=f32)
    t = pos / L
    bands = (FILT_EMB - 1) // 2
    f = jnp.linspace(1e-4, bands - 1, bands, dtype=f32)
    w = 2.0 * math.pi * t[:, None] * f[None, :]
    feats = jnp.concatenate([t[:, None], jnp.cos(w), jnp.sin(w)], axis=-1)
    tau = jnp.abs(pos - (L // 2)) / max(L // 2, 1)
    max_decay = math.log(DECAY_TARGET) / FAST_DECAY
    min_decay = math.log(DECAY_TARGET) / SLOW_DECAY
    deltas = jnp.abs(jnp.linspace(min_decay, max_decay, HY_W, dtype=f32))
    planes = lambda t: t.reshape(L // DFT_MINOR, DFT_MINOR, -1).transpose(1, 0, 2).reshape(L, -1)
    return planes(_pad_last(feats, LANES)), planes(tau[:, None]), jnp.tile(deltas, HY_ORDER)[None, :]


def _prep_weights(p):
    L = p["w_in"].shape[0]
    D = p["w_in"].shape[1]
    w_in = p["w_in"]
    kr = jnp.zeros((L, D, LANES), f32).at[:, :, QK_NOPE:QK_DIM].set(w_in[:, :, COL_KROPE:COL_HYENA])
    wq = _pad_last(p["w_q_b"].reshape(L, Q_LORA, N_HEADS, QK_DIM), LANES).reshape(L, Q_LORA, N_HEADS * LANES)
    wkv = p["w_kv_b"].reshape(L, KV_LORA, N_HEADS, QK_NOPE + V_DIM)
    wk = _pad_last(wkv[..., :QK_NOPE], LANES).reshape(L, KV_LORA, N_HEADS * LANES)
    wv = _pad_last(wkv[..., QK_NOPE:], LANES).reshape(L, KV_LORA, N_HEADS * LANES)
    padh = LANES - FILT_HID
    return dict(
        norm1_w=p["norm1_w"][:, None, :],
        wa=jnp.concatenate([w_in[:, :, COL_Q:COL_KROPE], kr], axis=-1).astype(bf16),
        wu=w_in[:, :, COL_HYENA:COL_GATE].astype(bf16),
        wg=w_in[:, :, COL_GATE:].astype(bf16),
        qaw=p["q_a_norm_w"][:, None, :], kvaw=p["kv_a_norm_w"][:, None, :],
        wq=wq.astype(bf16), wk=wk.astype(bf16), wv=wv.astype(bf16),
        qnw=_pad_last(p["q_norm_w"], LANES)[:, None, :], knw=_pad_last(p["k_norm_w"], LANES)[:, None, :],
        scw=p["short_conv_w"], scb=p["short_conv_b"][:, None, :],
        fw1=jnp.pad(p["filt_w1"], ((0, 0), (0, LANES - FILT_EMB), (0, padh))),
        fb1=_pad_last(p["filt_b1"], LANES)[:, None, :], ff1=_pad_last(p["filt_freq1"], LANES)[:, None, :],
        fw2=jnp.pad(p["filt_w2"], ((0, 0), (0, padh), (0, padh))),
        fb2=_pad_last(p["filt_b2"], LANES)[:, None, :], ff2=_pad_last(p["filt_freq2"], LANES)[:, None, :],
        fw3=jnp.pad(p["filt_w3"], ((0, 0), (0, padh), (0, 0))),
        skip=p["filt_skip"],
        w_attn=p["w_attn_branch"].astype(bf16), w_hy=p["w_hyena_branch"].astype(bf16),
        w_out=p["w_out"].astype(bf16),
        norm2_w=p["norm2_w"][:, None, :], wr=_pad_last(p["router_w"], LANES),
        ewg=p["expert_w_gate"].astype(bf16), ewu=p["expert_w_up"].astype(bf16),
        ewd=p["expert_w_down"].astype(bf16),
    )


def _layer(x, w, consts):
    B, S, D = x.shape
    T = B * S
    E = N_EXPERTS
    cap = max(1, EC_FACTOR * S // E)
    nh = S // DFT_MINOR

    a, u, g = _in_proj(x.reshape(T, D), w["norm1_w"], w["wa"], w["wu"], w["wg"])

    q, k, v = _mla_prep(a.reshape(B, S, -1), w["qaw"], w["kvaw"], w["wq"], w["wk"], w["wv"],
                        w["qnw"], w["knw"], consts["ct"], consts["s1"], consts["s2"])
    attn = _flash(q, k, v)
    attn = attn.transpose(0, 2, 1, 3).reshape(T, N_HEADS * V_DIM)

    ut = u.reshape(B, nh, DFT_MINOR, -1).transpose(0, 2, 1, 3)
    uct = _short_conv(ut, w["scw"], w["scb"])
    h_un, l1 = _filters(consts["feats"], consts["tau"], w["fw1"], w["fb1"], w["ff1"], w["fw2"], w["fb2"],
                        w["ff2"], w["fw3"], consts["deltas"])
    tb = consts["dft"]
    ha = _dft_a(h_un.reshape(DFT_MINOR, nh, HY_ORDER * HY_W), 0, HY_ORDER * HY_W,
                tb["m1"], tb["twc"], tb["tws"], packed=False)
    hspec = _dft_b_filter(ha, tb["m2f"], 1.0 / l1)
    z4, z_blk = uct, 0
    for n in range(HY_ORDER):
        za = _dft_a(z4, z_blk, HY_W, tb["m1"], tb["twc"], tb["tws"], packed=True)
        zb = _dft_b(za, hspec, n, tb["m2f"], tb["m2i"], tb["twc_t"], tb["tws_t"])
        z4 = _dft_c(zb, tb["gm"], z4, z_blk, uct, n + 1, w["skip"][n:n + 1])
        z_blk = 0
    hy = z4.transpose(0, 2, 1, 3).reshape(T, HY_W)

    x2, hn, aff = _merge(x.reshape(T, D), attn, hy, g, w["w_attn"], w["w_hy"], w["w_out"], w["norm2_w"], w["wr"])

    aff3 = aff.reshape(B, S, LANES)
    aff_t = aff3[:, :, :E].transpose(0, 2, 1)
    thr, need = _thresholds(aff_t.reshape(B * E, S), cap)
    pad_inf = jnp.full((B, LANES - E), jnp.inf, f32)
    thr3 = jnp.concatenate([thr[:, 0].reshape(B, E), pad_inf], axis=1)[:, None, :]
    need3 = _pad_last(need[:, 0].reshape(B, E), LANES)[:, None, :]
    sel, pos, base = _route(aff3, thr3, need3)
    base_flat = base[:, :, :E].transpose(0, 2, 1).reshape(-1).astype(jnp.int32)
    to_rows = lambda t: t[:, :, :E].transpose(0, 2, 1)[:, :, None, :]
    xs, gs = _gather(base_flat, hn.reshape(B, S, D), to_rows(pos), to_rows(sel), aff_t[:, :, None, :], cap)
    o = _ffn(xs, gs, w["ewg"], w["ewu"], w["ewd"])
    return _scatter(base_flat, x2.reshape(B, S, D), pos, sel, o)


def kernel(x, norm1_w, w_in, q_a_norm_w, w_q_b, kv_a_norm_w, w_kv_b, q_norm_w, k_norm_w, short_conv_w,
           short_conv_b, filt_w1, filt_b1, filt_freq1, filt_w2, filt_b2, filt_freq2, filt_w3, filt_skip,
           w_attn_branch, w_hyena_branch, w_out, norm2_w, router_w, expert_w_gate, expert_w_up, expert_w_down):
    B, S, D = x.shape
    assert S % (2 * DFT_MINOR) == 0 and w_hyena_branch.shape[1] == HY_W
    params = dict(norm1_w=norm1_w, w_in=w_in, q_a_norm_w=q_a_norm_w, w_q_b=w_q_b, kv_a_norm_w=kv_a_norm_w,
                  w_kv_b=w_kv_b, q_norm_w=q_norm_w, k_norm_w=k_norm_w, short_conv_w=short_conv_w,
                  short_conv_b=short_conv_b, filt_w1=filt_w1, filt_b1=filt_b1, filt_freq1=filt_freq1,
                  filt_w2=filt_w2, filt_b2=filt_b2, filt_freq2=filt_freq2, filt_w3=filt_w3, filt_skip=filt_skip,
                  w_attn_branch=w_attn_branch, w_hyena_branch=w_hyena_branch, w_out=w_out, norm2_w=norm2_w,
                  router_w=router_w, expert_w_gate=expert_w_gate, expert_w_up=expert_w_up,
                  expert_w_down=expert_w_down)
    weights = _prep_weights(params)
    ct, s1, s2 = _rope_tables(S)
    feats, tau, deltas = _hyena_positions(S)
    consts = dict(ct=ct, s1=s1, s2=s2, feats=feats, tau=tau, deltas=deltas, dft=_dft_tables(S))

    def body(xc, w):
        return _layer(xc, w, consts), None

    out, _ = lax.scan(body, x, weights)
    return out
```

```python
import functools
import math

import jax
import jax.numpy as jnp
from jax import lax
from jax.experimental import pallas as pl
from jax.experimental.pallas import tpu as pltpu

f32 = jnp.float32
bf16 = jnp.bfloat16

N_HEADS = 16
QK_NOPE = 64
QK_ROPE = 32
QK_DIM = QK_NOPE + QK_ROPE
V_DIM = 64
Q_LORA = 384
KV_LORA = 256
ROPE_THETA = 10000.0
HY_W = 1024
HY_ORDER = 2
FILT_EMB = 33
FILT_HID = 64
DECAY_TARGET = 1e-2
FAST_DECAY = 0.3
SLOW_DECAY = 1.5
N_EXPERTS = 16
EC_FACTOR = 2
RMS_EPS = 1e-6

COL_Q = 0
COL_KV = COL_Q + Q_LORA
COL_KROPE = COL_KV + KV_LORA
COL_HYENA = COL_KROPE + QK_ROPE
COL_GATE = COL_HYENA + (HY_ORDER + 1) * HY_W

LANES = 128
SUBLANES = 8
DFT_MINOR = 128
GATHER_WIN = 144
DFT_STEP_PLANES = 2
VT_ROWS = 80
VMEM_LIMIT = 56 * 1024 * 1024


def _cp(sem, vmem=VMEM_LIMIT):
    return pltpu.CompilerParams(dimension_semantics=sem, vmem_limit_bytes=vmem)


def _tile(n, pref):
    t = min(n, pref)
    assert n % t == 0, (n, pref)
    return t


def _dot(a, b):
    return jnp.dot(a, b, preferred_element_type=f32)


def _dot_hi(a, b):
    return jnp.dot(a, b, preferred_element_type=f32, precision=lax.Precision.HIGHEST)


def _in_proj_kernel(x_ref, nw_ref, wa_ref, wu_ref, wg_ref, a_ref, u_ref, g_ref):
    x = x_ref[...]
    ms = jnp.mean(x * x, axis=-1, keepdims=True)
    xn = (x * lax.rsqrt(ms + RMS_EPS) * nw_ref[...]).astype(bf16)
    a_ref[...] = _dot(xn, wa_ref[...])
    u_ref[...] = _dot(xn, wu_ref[...]).astype(bf16)
    g_ref[...] = jax.nn.sigmoid(_dot(xn, wg_ref[...])).astype(bf16)


def _in_proj(x2, nw, wa, wu, wg):
    T, D = x2.shape
    tm = _tile(T, 256)
    na, nu, ng = wa.shape[1], wu.shape[1], wg.shape[1]
    full = lambda i: (0, 0)
    row = lambda i: (i, 0)
    return pl.pallas_call(
        _in_proj_kernel,
        grid=(T // tm,),
        in_specs=[pl.BlockSpec((tm, D), row), pl.BlockSpec((1, D), full),
                  pl.BlockSpec((D, na), full), pl.BlockSpec((D, nu), full), pl.BlockSpec((D, ng), full)],
        out_specs=[pl.BlockSpec((tm, na), row), pl.BlockSpec((tm, nu), row), pl.BlockSpec((tm, ng), row)],
        out_shape=[jax.ShapeDtypeStruct((T, na), f32), jax.ShapeDtypeStruct((T, nu), bf16),
                   jax.ShapeDtypeStruct((T, ng), bf16)],
        compiler_params=_cp(("parallel",)),
        name="in_proj",
    )(x2, nw, wa, wu, wg)


def _head_norm_rope(t, w, ct, s1, s2):
    ss = jnp.sum(t * t, axis=-1, keepdims=True) * (1.0 / QK_DIM)
    t = t * lax.rsqrt(ss + RMS_EPS) * w
    return t * ct + pltpu.roll(t, LANES - QK_ROPE // 2, 1) * s1 + pltpu.roll(t, QK_ROPE // 2, 1) * s2


def _mla_prep_kernel(a_ref, qaw_ref, kvaw_ref, wq_ref, wk_ref, wv_ref, qnw_ref, knw_ref,
                     ct_ref, s1_ref, s2_ref, q_ref, k_ref, v_ref, *, qscale):
    a = a_ref[...]
    ql = a[:, :Q_LORA]
    kvl = a[:, Q_LORA:Q_LORA + KV_LORA]
    kr = a[:, Q_LORA + KV_LORA:]
    qn = (ql * lax.rsqrt(jnp.mean(ql * ql, axis=-1, keepdims=True) + RMS_EPS) * qaw_ref[...]).astype(bf16)
    kvn = (kvl * lax.rsqrt(jnp.mean(kvl * kvl, axis=-1, keepdims=True) + RMS_EPS) * kvaw_ref[...]).astype(bf16)
    q = _dot(qn, wq_ref[...])
    k = _dot(kvn, wk_ref[...])
    v = _dot(kvn, wv_ref[...])
    ct, s1, s2 = ct_ref[...], s1_ref[...], s2_ref[...]
    lane = lax.broadcasted_iota(jnp.int32, (1, LANES), 1)
    ones_col = (lane == V_DIM).astype(f32)
    for h in range(N_HEADS):
        sl = slice(h * LANES, (h + 1) * LANES)
        qh = _head_norm_rope(q[:, sl], qnw_ref[...], ct, s1, s2) * qscale
        kh = _head_norm_rope(k[:, sl] + kr, knw_ref[...], ct, s1, s2)
        q_ref[h] = qh.astype(bf16)
        k_ref[h] = kh.astype(bf16)
        v_ref[h] = (v[:, sl] + ones_col).T[:VT_ROWS].astype(bf16)


def _mla_prep(a3, qaw, kvaw, wq, wk, wv, qnw, knw, ct, s1, s2):
    B, S, NA = a3.shape
    tm = _tile(S, 512)
    HP = N_HEADS * LANES
    qscale = (QK_DIM ** -0.5) * math.log2(math.e)
    full2 = lambda b, i: (0, 0)
    pos = lambda b, i: (i, 0)
    out_spec = pl.BlockSpec((None, N_HEADS, tm, LANES), lambda b, i: (b, 0, i, 0))
    out_sds = jax.ShapeDtypeStruct((B, N_HEADS, S, LANES), bf16)
    return pl.pallas_call(
        functools.partial(_mla_prep_kernel, qscale=qscale),
        grid=(B, S // tm),
        in_specs=[pl.BlockSpec((None, tm, NA), lambda b, i: (b, i, 0)),
                  pl.BlockSpec((1, Q_LORA), full2), pl.BlockSpec((1, KV_LORA), full2),
                  pl.BlockSpec((Q_LORA, HP), full2), pl.BlockSpec((KV_LORA, HP), full2),
                  pl.BlockSpec((KV_LORA, HP), full2),
                  pl.BlockSpec((1, LANES), full2), pl.BlockSpec((1, LANES), full2),
                  pl.BlockSpec((tm, LANES), pos), pl.BlockSpec((tm, LANES), pos), pl.BlockSpec((tm, LANES), pos)],
        out_specs=[out_spec, out_spec,
                   pl.BlockSpec((None, N_HEADS, None, VT_ROWS, tm), lambda b, i: (b, 0, i, 0, 0))],
        out_shape=[out_sds, out_sds, jax.ShapeDtypeStruct((B, N_HEADS, S // tm, VT_ROWS, tm), bf16)],
        compiler_params=_cp(("parallel", "parallel")),
        name="mla_prep",
    )(a3, qaw, kvaw, wq, wk, wv, qnw, knw, ct, s1, s2)


def _flash_kernel(q_ref, k_ref, vt_ref, o_ref, m_sc, acc_sc, s0_sc, s1_sc, *, unroll, online_max):
    nchunk, _, ck = vt_ref.shape
    q = q_ref[...]
    m_sc[...] = jnp.full(m_sc.shape, -jnp.inf, f32)
    acc_sc[...] = jnp.zeros(acc_sc.shape, f32)
    sbuf = (s0_sc, s1_sc)

    def scores(c, buf):
        start = pl.multiple_of(c * ck, ck)
        buf[...] = lax.dot_general(k_ref[pl.ds(start, ck), :], q, (((1,), (1,)), ((), ())),
                                   preferred_element_type=f32)

    def softmax_pv(c, buf):
        st = buf[...]
        if online_max:
            m = m_sc[...]
            m_new = jnp.maximum(m, jnp.max(st, axis=0, keepdims=True))
            pt = jnp.exp2(st - m_new).astype(bf16)
            acc_sc[0:VT_ROWS, :] = jnp.exp2(m - m_new) * acc_sc[0:VT_ROWS, :] + _dot(vt_ref[c], pt)
            m_sc[...] = m_new
        else:
            acc_sc[0:VT_ROWS, :] += _dot(vt_ref[c], jnp.exp2(st).astype(bf16))

    scores(0, s0_sc)

    def body(i, carry):
        for u in range(unroll):
            c = i * unroll + u
            scores(jnp.minimum(c + 1, nchunk - 1), sbuf[(u + 1) % 2])
            softmax_pv(c, sbuf[u % 2])
        return carry

    lax.fori_loop(0, nchunk // unroll, body, 0)
    acc = acc_sc[...].T
    o_ref[...] = (acc[:, :V_DIM] / acc[:, V_DIM:V_DIM + 1]).astype(o_ref.dtype)


SCORE_BOUND_LIMIT = 90.0


def _flash(q, k, vt, score_bound):
    B, H, S, _ = q.shape
    nchunk, ck = vt.shape[2], vt.shape[4]
    tq = _tile(S, 512)
    unroll = 8 if nchunk % 8 == 0 else (4 if nchunk % 4 == 0 else 2)
    assert nchunk % unroll == 0

    def call(online_max):
        return pl.pallas_call(
            functools.partial(_flash_kernel, unroll=unroll, online_max=online_max),
            grid=(B, H, S // tq),
            in_specs=[pl.BlockSpec((None, None, tq, LANES), lambda b, h, i: (b, h, i, 0)),
                      pl.BlockSpec((None, None, S, LANES), lambda b, h, i: (b, h, 0, 0)),
                      pl.BlockSpec((None, None, nchunk, VT_ROWS, ck), lambda b, h, i: (b, h, 0, 0, 0))],
            out_specs=pl.BlockSpec((None, None, tq, V_DIM), lambda b, h, i: (b, h, i, 0)),
            out_shape=jax.ShapeDtypeStruct((B, H, S, V_DIM), bf16),
            scratch_shapes=[pltpu.VMEM((1, tq), f32), pltpu.VMEM((LANES, tq), f32),
                            pltpu.VMEM((ck, tq), f32), pltpu.VMEM((ck, tq), f32)],
            compiler_params=_cp(("parallel", "parallel", "parallel")),
            name="flash_attn_online" if online_max else "flash_attn_bounded")

    return lax.cond(score_bound <= SCORE_BOUND_LIMIT, call(False), call(True), q, k, vt)


def _short_conv_kernel(u_ref, prev_ref, next_ref, w_ref, b_ref, o_ref):
    jb = pl.program_id(1)
    last = pl.num_programs(1) - 1
    nplanes, nh, _ = u_ref.shape
    rows = lax.broadcasted_iota(jnp.int32, (nh, 1), 0)
    prev = prev_ref[...].astype(f32)
    nxt = next_ref[...].astype(f32)
    prev_wrap = jnp.where(rows == 0, 0.0, pltpu.roll(prev, 1, 0))
    next_wrap = jnp.where(rows == nh - 1, 0.0, pltpu.roll(nxt, nh - 1, 0))
    halo_up = jnp.where(jb == 0, prev_wrap, prev)
    halo_dn = jnp.where(jb == last, next_wrap, nxt)
    w = w_ref[...]
    bias = b_ref[...]
    for p in range(nplanes):
        up = halo_up if p == 0 else u_ref[p - 1].astype(f32)
        dn = halo_dn if p == nplanes - 1 else u_ref[p + 1].astype(f32)
        o_ref[p] = (bias + up * w[0:1, :] + u_ref[p].astype(f32) * w[1:2, :] + dn * w[2:3, :]).astype(o_ref.dtype)


def _short_conv(ut, w, b):
    B, P, nh, CT = ut.shape
    tc = _tile(CT, 1024)
    J = 8
    blk = pl.BlockSpec((None, J, nh, tc), lambda bi, j, c: (bi, j, 0, c))
    halo = lambda off: pl.BlockSpec((None, None, nh, tc), lambda bi, j, c: (bi, (j * J + off + P) % P, 0, c))
    return pl.pallas_call(
        _short_conv_kernel,
        grid=(B, P // J, CT // tc),
        in_specs=[blk, halo(-1), halo(J),
                  pl.BlockSpec((3, tc), lambda bi, j, c: (0, c)),
                  pl.BlockSpec((1, tc), lambda bi, j, c: (0, c))],
        out_specs=blk,
        out_shape=jax.ShapeDtypeStruct((B, P, nh, CT), bf16),
        compiler_params=_cp(("parallel", "parallel", "parallel")),
        name="short_conv",
    )(ut, ut, ut, w, b)


def _filter_kernel(feat_ref, tau_ref, w1_ref, b1_ref, f1_ref, w2_ref, b2_ref, f2_ref, w3_ref, dl_ref,
                   h_ref, l1_ref):
    i = pl.program_id(0)
    h = jnp.sin(f1_ref[...] * (_dot_hi(feat_ref[...], w1_ref[...]) + b1_ref[...]))
    h = jnp.sin(f2_ref[...] * (_dot_hi(h, w2_ref[...]) + b2_ref[...]))
    h = _dot_hi(h, w3_ref[...])
    h = h * jnp.exp(-tau_ref[...] * dl_ref[...])
    h_ref[...] = h

    @pl.when(i == 0)
    def _():
        l1_ref[...] = jnp.zeros(l1_ref.shape, f32)

    l1_ref[...] += jnp.sum(jnp.abs(h), axis=0, keepdims=True)


def _filters(feats, tau, w1, b1, f1, w2, b2, f2, w3, dl):
    S = feats.shape[0]
    CO = w3.shape[1]
    tm = _tile(S, 512)
    full = lambda i: (0, 0)
    row = lambda i: (i, 0)
    return pl.pallas_call(
        _filter_kernel,
        grid=(S // tm,),
        in_specs=[pl.BlockSpec((tm, LANES), row), pl.BlockSpec((tm, 1), row),
                  pl.BlockSpec((LANES, LANES), full), pl.BlockSpec((1, LANES), full), pl.BlockSpec((1, LANES), full),
                  pl.BlockSpec((LANES, LANES), full), pl.BlockSpec((1, LANES), full), pl.BlockSpec((1, LANES), full),
                  pl.BlockSpec((LANES, CO), full), pl.BlockSpec((1, CO), full)],
        out_specs=[pl.BlockSpec((tm, CO), row), pl.BlockSpec((1, CO), full)],
        out_shape=[jax.ShapeDtypeStruct((S, CO), f32), jax.ShapeDtypeStruct((1, CO), f32)],
        compiler_params=_cp(("arbitrary",)),
        name="hyena_filters",
    )(feats, tau, w1, b1, f1, w2, b2, f2, w3, dl)


def _dft_a_kernel(x_ref, m_ref, tc_ref, ts_ref, a_ref, *, packed):
    n1f = m_ref.shape[0] // 2
    m = m_ref[...]
    for j in range(tc_ref.shape[0]):
        if packed:
            pr = _dot(m, x_ref[0, j].astype(bf16))
            pi = _dot(m, x_ref[1, j].astype(bf16))
            ar = pr[:n1f] + pi[n1f:]
            ai = pi[:n1f] - pr[n1f:]
        else:
            p = _dot(m, x_ref[j].astype(bf16))
            ar = p[:n1f]
            ai = -p[n1f:]
        tc, ts = tc_ref[j], ts_ref[j]
        a_ref[0, j] = (ar * tc + ai * ts).astype(a_ref.dtype)
        a_ref[1, j] = (ai * tc - ar * ts).astype(a_ref.dtype)


def _dft_a(x4, col_blk, C, m1, twc, tws, packed):
    n1f = m1.shape[0] // 2
    nh = m1.shape[1]
    J = DFT_STEP_PLANES
    if packed:
        x_spec = pl.BlockSpec((2, J, nh, C), lambda j: (0, j, 0, col_blk))
    else:
        x_spec = pl.BlockSpec((J, nh, C), lambda j: (j, 0, col_blk))
    return pl.pallas_call(
        functools.partial(_dft_a_kernel, packed=packed),
        grid=(DFT_MINOR // J,),
        in_specs=[x_spec,
                  pl.BlockSpec((2 * n1f, nh), lambda j: (0, 0)),
                  pl.BlockSpec((J, n1f, 1), lambda j: (j, 0, 0)),
                  pl.BlockSpec((J, n1f, 1), lambda j: (j, 0, 0))],
        out_specs=pl.BlockSpec((2, J, n1f, C), lambda j: (0, j, 0, 0)),
        out_shape=jax.ShapeDtypeStruct((2, DFT_MINOR, n1f, C), bf16),
        compiler_params=_cp(("parallel",)),
        name="dft_stage_a",
    )(x4, m1, twc, tws)


def _dft_bf_kernel(a_ref, m2_ref, inv_ref, h_ref):
    C = a_ref.shape[-1]
    for j in range(a_ref.shape[1]):
        a = a_ref[:, j].reshape(2 * DFT_MINOR, C)
        z = _dot(m2_ref[...], a) * inv_ref[...]
        h_ref[j] = z.reshape(2, DFT_MINOR, C).astype(h_ref.dtype)


def _dft_b_filter(a4, m2f, inv_l1):
    _, _, n1f, C = a4.shape
    a4 = a4.swapaxes(1, 2)
    J = DFT_STEP_PLANES
    return pl.pallas_call(
        _dft_bf_kernel,
        grid=(n1f // J,),
        in_specs=[pl.BlockSpec((2, J, DFT_MINOR, C), lambda k: (0, k, 0, 0)),
                  pl.BlockSpec((2 * DFT_MINOR, 2 * DFT_MINOR), lambda k: (0, 0)),
                  pl.BlockSpec((1, C), lambda k: (0, 0))],
        out_specs=pl.BlockSpec((J, 2, DFT_MINOR, C), lambda k: (k, 0, 0, 0)),
        out_shape=jax.ShapeDtypeStruct((n1f, 2, DFT_MINOR, C), bf16),
        compiler_params=_cp(("parallel",)),
        name="dft_stage_b_filter",
    )(a4, m2f, inv_l1)


def _dft_b_kernel(a_ref, h_ref, m2f_ref, m2i_ref, tc_ref, ts_ref, b_ref):
    C = a_ref.shape[-1]
    for j in range(a_ref.shape[1]):
        a = a_ref[:, j].reshape(2 * DFT_MINOR, C)
        z = _dot(m2f_ref[...], a)
        zr, zi = z[:DFT_MINOR], z[DFT_MINOR:]
        hr, hi = h_ref[j, 0].astype(f32), h_ref[j, 1].astype(f32)
        y = jnp.concatenate([zr * hr - zi * hi, zr * hi + zi * hr], axis=0).astype(bf16)
        w = _dot(m2i_ref[...], y)
        wr, wi = w[:DFT_MINOR], w[DFT_MINOR:]
        tc, ts = tc_ref[j], ts_ref[j]
        b_ref[0, j] = (wr * tc - wi * ts).astype(b_ref.dtype)
        b_ref[1, j] = (wr * ts + wi * tc).astype(b_ref.dtype)


def _dft_b(a4, hspec, order, m2f, m2i, twc_t, tws_t):
    _, _, n1f, C = a4.shape
    a4 = a4.swapaxes(1, 2)
    J = DFT_STEP_PLANES
    return pl.pallas_call(
        _dft_b_kernel,
        grid=(n1f // J,),
        in_specs=[pl.BlockSpec((2, J, DFT_MINOR, C), lambda k: (0, k, 0, 0)),
                  pl.BlockSpec((J, 2, DFT_MINOR, C), lambda k: (k, 0, 0, order)),
                  pl.BlockSpec((2 * DFT_MINOR, 2 * DFT_MINOR), lambda k: (0, 0)),
                  pl.BlockSpec((2 * DFT_MINOR, 2 * DFT_MINOR), lambda k: (0, 0)),
                  pl.BlockSpec((J, DFT_MINOR, 1), lambda k: (k, 0, 0)),
                  pl.BlockSpec((J, DFT_MINOR, 1), lambda k: (k, 0, 0))],
        out_specs=pl.BlockSpec((2, J, DFT_MINOR, C), lambda k: (0, k, 0, 0)),
        out_shape=jax.ShapeDtypeStruct((2, n1f, DFT_MINOR, C), bf16),
        compiler_params=_cp(("parallel",)),
        name="dft_stage_b",
    )(a4, hspec, m2f, m2i, twc_t, tws_t)


def _dft_c_kernel(b_ref, g_ref, z_ref, xg_ref, skip_ref, o_ref):
    n1f = b_ref.shape[2]
    C = b_ref.shape[-1]
    nh = n1f // 2
    skip = skip_ref[...]
    for j in range(b_ref.shape[1]):
        bb = b_ref[:, j].reshape(2 * n1f, C)
        y = _dot(g_ref[...], bb)
        for bi in range(2):
            z = z_ref[bi, j].astype(f32)
            o_ref[bi, j] = (xg_ref[bi, j].astype(f32) * (y[bi * nh:(bi + 1) * nh] + z * skip)).astype(o_ref.dtype)


def _dft_c(b4, gm, z4, z_blk, xg4, xg_blk, skip):
    _, n1f, _, C = b4.shape
    nh = n1f // 2
    b4 = b4.swapaxes(1, 2)
    J = DFT_STEP_PLANES
    return pl.pallas_call(
        _dft_c_kernel,
        grid=(DFT_MINOR // J,),
        in_specs=[pl.BlockSpec((2, J, n1f, C), lambda j: (0, j, 0, 0)),
                  pl.BlockSpec((n1f, 2 * n1f), lambda j: (0, 0)),
                  pl.BlockSpec((2, J, nh, C), lambda j: (0, j, 0, z_blk)),
                  pl.BlockSpec((2, J, nh, C), lambda j: (0, j, 0, xg_blk)),
                  pl.BlockSpec((1, C), lambda j: (0, 0))],
        out_specs=pl.BlockSpec((2, J, nh, C), lambda j: (0, j, 0, 0)),
        out_shape=jax.ShapeDtypeStruct((2, DFT_MINOR, nh, C), bf16),
        compiler_params=_cp(("parallel",)),
        name="dft_stage_c",
    )(b4, gm, z4, xg4, skip)


def _dft_tables(S):
    N = 2 * S
    n1f = N // DFT_MINOR
    nh = n1f // 2
    two_pi = 2.0 * math.pi

    def cs(num, den):
        ang = (num % den).astype(f32) * (two_pi / den)
        return jnp.cos(ang), jnp.sin(ang)

    k1 = jnp.arange(n1f, dtype=jnp.int32)
    n1 = jnp.arange(nh, dtype=jnp.int32)
    c1, s1 = cs(k1[:, None] * n1[None, :], n1f)
    m1 = jnp.concatenate([c1, s1], axis=0).astype(bf16)
    n2 = jnp.arange(DFT_MINOR, dtype=jnp.int32)
    twc, tws = cs(n2[:, None] * k1[None, :], N)
    c2, s2 = cs(n2[:, None] * n2[None, :], DFT_MINOR)
    m2f = jnp.concatenate([jnp.concatenate([c2, s2], 1), jnp.concatenate([-s2, c2], 1)], 0).astype(bf16)
    m2i = jnp.concatenate([jnp.concatenate([c2, -s2], 1), jnp.concatenate([s2, c2], 1)], 0).astype(bf16)
    rows = jnp.arange(nh, dtype=jnp.int32) + n1f // 4
    gc, gs = cs(rows[:, None] * k1[None, :], n1f)
    gm = (jnp.concatenate([jnp.concatenate([gc, -gs], 1), jnp.concatenate([gs, gc], 1)], 0) / N).astype(bf16)
    return dict(m1=m1, twc=twc[:, :, None], tws=tws[:, :, None],
                twc_t=twc.T[:, :, None], tws_t=tws.T[:, :, None], m2f=m2f, m2i=m2i, gm=gm)


def _merge_kernel(x_ref, at_ref, hy_ref, g_ref, wa_ref, wh_ref, wo_ref, n2_ref, wr_ref,
                  xo_ref, hn_ref, aff_ref):
    D = x_ref.shape[1]
    g = g_ref[...].astype(f32)
    merged = g[:, :D] * _dot(at_ref[...], wa_ref[...]) + g[:, D:] * _dot(hy_ref[...], wh_ref[...])
    x = x_ref[...] + _dot(merged.astype(bf16), wo_ref[...])
    xo_ref[...] = x
    hn = x * lax.rsqrt(jnp.mean(x * x, axis=-1, keepdims=True) + RMS_EPS) * n2_ref[...]
    hn_ref[...] = hn.astype(bf16)
    logits = _dot_hi(hn, wr_ref[...])
    lane = lax.broadcasted_iota(jnp.int32, logits.shape, 1)
    logits = jnp.where(lane < N_EXPERTS, logits, -jnp.inf)
    e = jnp.exp(logits - jnp.max(logits, axis=-1, keepdims=True))
    aff_ref[...] = e / jnp.sum(e, axis=-1, keepdims=True)


def _merge(x2, attn, hy, g, wa, wh, wo, n2w, wr):
    T, D = x2.shape
    tm = _tile(T, 256)
    row = lambda i: (i, 0)
    full = lambda i: (0, 0)
    return pl.pallas_call(
        _merge_kernel,
        grid=(T // tm,),
        in_specs=[pl.BlockSpec((tm, D), row), pl.BlockSpec((tm, D), row), pl.BlockSpec((tm, D), row),
                  pl.BlockSpec((tm, 2 * D), row),
                  pl.BlockSpec((D, D), full), pl.BlockSpec((D, D), full), pl.BlockSpec((D, D), full),
                  pl.BlockSpec((1, D), full), pl.BlockSpec((D, LANES), full)],
        out_specs=[pl.BlockSpec((tm, D), row), pl.BlockSpec((tm, D), row), pl.BlockSpec((tm, LANES), row)],
        out_shape=[jax.ShapeDtypeStruct((T, D), f32), jax.ShapeDtypeStruct((T, D), bf16),
                   jax.ShapeDtypeStruct((T, LANES), f32)],
        compiler_params=_cp(("parallel",)),
        name="merge_out_router",
    )(x2, attn, hy, g, wa, wh, wo, n2w, wr)


def _thresh_kernel(a_ref, thr_ref, need_ref, *, cap):
    bits = pltpu.bitcast(a_ref[...], jnp.int32)
    R = bits.shape[0]

    def body(i, thr):
        cand = thr | jnp.left_shift(jnp.int32(1), 30 - i)
        cnt = jnp.sum((bits >= cand).astype(f32), axis=1, keepdims=True)
        return jnp.where(cnt >= cap, cand, thr)

    thr = lax.fori_loop(0, 31, body, jnp.zeros((R, 1), jnp.int32))
    n_gt = jnp.sum((bits > thr).astype(f32), axis=1, keepdims=True)
    thr_ref[...] = jnp.broadcast_to(pltpu.bitcast(thr, f32), thr_ref.shape)
    need_ref[...] = jnp.broadcast_to(cap - n_gt, need_ref.shape)


def _thresholds(aff_t, cap):
    R, S = aff_t.shape
    return pl.pallas_call(
        functools.partial(_thresh_kernel, cap=float(cap)),
        out_shape=[jax.ShapeDtypeStruct((R, LANES), f32), jax.ShapeDtypeStruct((R, LANES), f32)],
        compiler_params=_cp(None),
        name="route_threshold",
    )(aff_t)


def _route_kernel(aff_ref, thr_ref, need_ref, ltri_ref, slab_ref, sel_ref, pos_ref, base_ref, ceq_sc, csel_sc):
    k = pl.program_id(1)

    @pl.when(k == 0)
    def _():
        ceq_sc[...] = jnp.zeros(ceq_sc.shape, f32)
        csel_sc[...] = jnp.zeros(csel_sc.shape, f32)

    a = aff_ref[...]
    thr = thr_ref[...]
    gt = a > thr
    eq = a == thr
    eqf = eq.astype(f32)
    ltri = ltri_ref[...]
    eq_rank = ceq_sc[...] + _dot(ltri, eqf.astype(bf16))
    sel = jnp.logical_or(gt, jnp.logical_and(eq, eq_rank < need_ref[...]))
    self_ = sel.astype(f32)
    pos = csel_sc[...] + _dot(ltri, self_.astype(bf16))
    sel_ref[...] = self_
    pos_ref[...] = pos
    base_ref[...] = csel_sc[...] + _dot(slab_ref[...], self_.astype(bf16))
    ceq_sc[...] += jnp.sum(eqf, axis=0, keepdims=True)
    csel_sc[...] += jnp.sum(self_, axis=0, keepdims=True)


def _route(aff3, thr3, need3):
    B, S, _ = aff3.shape
    tk = _tile(S, 1024)
    ltri = (jnp.arange(tk)[:, None] > jnp.arange(tk)[None, :]).astype(bf16)
    tok = pl.BlockSpec((None, tk, LANES), lambda b, k: (b, k, 0))
    per_b = pl.BlockSpec((None, 1, LANES), lambda b, k: (b, 0, 0))
    return pl.pallas_call(
        _route_kernel,
        grid=(B, S // tk),
        in_specs=[tok, per_b, per_b, pl.BlockSpec((tk, tk), lambda b, k: (0, 0)),
                  pl.BlockSpec((tk // LANES, tk), lambda b, k: (0, 0))],
        out_specs=[tok, tok, pl.BlockSpec((None, tk // LANES, LANES), lambda b, k: (b, k, 0))],
        out_shape=[jax.ShapeDtypeStruct((B, S, LANES), f32), jax.ShapeDtypeStruct((B, S, LANES), f32),
                   jax.ShapeDtypeStruct((B, S // LANES, LANES), f32)],
        scratch_shapes=[pltpu.VMEM((1, LANES), f32), pltpu.VMEM((1, LANES), f32)],
        compiler_params=_cp(("parallel", "arbitrary")),
        name="route_positions",
    )(aff3, thr3, need3, ltri, ltri[::LANES])


def _gather_kernel(base_sm, hn_ref, pos_ref, sel_ref, gate_ref, xs_ref, gs_ref, acc_sc, gacc_sc, *, nsub, nsub_tot):
    b, e, k = pl.program_id(0), pl.program_id(1), pl.program_id(2)
    cap = xs_ref.shape[0]

    @pl.when(k == 0)
    def _():
        acc_sc[...] = jnp.zeros(acc_sc.shape, f32)
        gacc_sc[...] = jnp.zeros(gacc_sc.shape, f32)

    rows = lax.broadcasted_iota(jnp.int32, (GATHER_WIN, LANES), 0).astype(f32)
    for j in range(nsub):
        base = base_sm[(b * N_EXPERTS + e) * nsub_tot + k * nsub + j]
        start = pl.multiple_of((base // SUBLANES) * SUBLANES, SUBLANES)
        sl = slice(j * LANES, (j + 1) * LANES)
        rel = pos_ref[:, sl] - start.astype(f32)
        hit = jnp.logical_and(rows == rel, sel_ref[:, sl] > 0.0)
        oht = jnp.where(hit, 1.0, 0.0)
        acc_sc[pl.ds(start, GATHER_WIN), :] += _dot(oht.astype(bf16), hn_ref[sl, :])
        gacc_sc[pl.ds(start, GATHER_WIN), :] += jnp.sum(oht * gate_ref[:, sl], axis=1, keepdims=True)

    @pl.when(k == pl.num_programs(2) - 1)
    def _():
        xs_ref[...] = acc_sc[0:cap, :].astype(xs_ref.dtype)
        gs_ref[...] = gacc_sc[0:cap, :]


def _gather(base_flat, hn3, pos_t, sel_t, gate_t, cap):
    B, S, D = hn3.shape
    tk = _tile(S, 1024)
    nsub = tk // LANES
    row = pl.BlockSpec((None, None, 1, tk), lambda b, e, k, base: (b, e, 0, k))
    return pl.pallas_call(
        functools.partial(_gather_kernel, nsub=nsub, nsub_tot=S // LANES),
        grid_spec=pltpu.PrefetchScalarGridSpec(
            num_scalar_prefetch=1,
            grid=(B, N_EXPERTS, S // tk),
            in_specs=[pl.BlockSpec((None, tk, D), lambda b, e, k, base: (b, k, 0)), row, row, row],
            out_specs=[pl.BlockSpec((None, None, cap, D), lambda b, e, k, base: (b, e, 0, 0)),
                       pl.BlockSpec((None, None, cap, 1), lambda b, e, k, base: (b, e, 0, 0))],
            scratch_shapes=[pltpu.VMEM((cap + GATHER_WIN, D), f32), pltpu.VMEM((cap + GATHER_WIN, 1), f32)]),
        out_shape=[jax.ShapeDtypeStruct((B, N_EXPERTS, cap, D), bf16),
                   jax.ShapeDtypeStruct((B, N_EXPERTS, cap, 1), f32)],
        compiler_params=_cp(("parallel", "parallel", "arbitrary")),
        name="moe_gather",
    )(base_flat, hn3, pos_t, sel_t, gate_t)


def _ffn_kernel(xs_ref, gs_ref, wg_ref, wu_ref, wd_ref, o_ref):
    xs = xs_ref[...]
    a = _dot(xs, wg_ref[...])
    u = _dot(xs, wu_ref[...])
    hmid = (a * jax.nn.sigmoid(a) * u).astype(bf16)
    o_ref[...] = (_dot(hmid, wd_ref[...]) * gs_ref[...]).astype(o_ref.dtype)


def _ffn(xs, gs, wg, wu, wd):
    B, E, cap, D = xs.shape
    F = wg.shape[-1]
    tm = _tile(cap, 512)
    return pl.pallas_call(
        _ffn_kernel,
        grid=(E, B, cap // tm),
        in_specs=[pl.BlockSpec((None, None, tm, D), lambda e, b, i: (b, e, i, 0)),
                  pl.BlockSpec((None, None, tm, 1), lambda e, b, i: (b, e, i, 0)),
                  pl.BlockSpec((None, D, F), lambda e, b, i: (e, 0, 0)),
                  pl.BlockSpec((None, D, F), lambda e, b, i: (e, 0, 0)),
                  pl.BlockSpec((None, F, D), lambda e, b, i: (e, 0, 0))],
        out_specs=pl.BlockSpec((None, None, tm, D), lambda e, b, i: (b, e, i, 0)),
        out_shape=jax.ShapeDtypeStruct((B, E, cap, D), bf16),
        compiler_params=_cp(("parallel", "parallel", "parallel")),
        name="expert_ffn",
    )(xs, gs, wg, wu, wd)


def _scatter_kernel(base_sm, x_ref, pos_ref, sel_ref, o_hbm, out_ref, buf, sem, *, nsub_tot, cap):
    g = pl.program_id(0)

    def window(step, e):
        b, k = step // nsub_tot, step % nsub_tot
        base = base_sm[(b * N_EXPERTS + e) * nsub_tot + k]
        start = jnp.minimum((base // SUBLANES) * SUBLANES, cap - GATHER_WIN)
        start = pl.multiple_of(start, SUBLANES)
        slot = step % 2
        return start, pltpu.make_async_copy(o_hbm.at[b, e, pl.ds(start, GATHER_WIN), :],
                                            buf.at[slot, e], sem.at[slot, e])

    def fetch(step):
        for e in range(N_EXPERTS):
            window(step, e)[1].start()

    @pl.when(g == 0)
    def _():
        fetch(g)

    @pl.when(g + 1 < pl.num_programs(0))
    def _():
        fetch(g + 1)

    acc = x_ref[...]
    pos = pos_ref[...]
    sel = sel_ref[...]
    lanes = lax.broadcasted_iota(jnp.int32, (LANES, GATHER_WIN), 1).astype(f32)
    for e in range(N_EXPERTS):
        start, cp = window(g, e)
        cp.wait()
        rel = pos[:, e:e + 1] - start.astype(f32)
        hit = jnp.logical_and(lanes == rel, sel[:, e:e + 1] > 0.0)
        acc = acc + _dot(jnp.where(hit, 1.0, 0.0).astype(bf16), buf[g % 2, e])
    out_ref[...] = acc


def _scatter(base_flat, x3, pos3, sel3, o4):
    B, S, D = x3.shape
    cap = o4.shape[2]
    nsub = S // LANES
    assert cap >= GATHER_WIN
    tok = lambda g, base: (g // nsub, g % nsub, 0)
    return pl.pallas_call(
        functools.partial(_scatter_kernel, nsub_tot=nsub, cap=cap),
        grid_spec=pltpu.PrefetchScalarGridSpec(
            num_scalar_prefetch=1,
            grid=(B * nsub,),
            in_specs=[pl.BlockSpec((None, LANES, D), tok), pl.BlockSpec((None, LANES, LANES), tok),
                      pl.BlockSpec((None, LANES, LANES), tok), pl.BlockSpec(memory_space=pl.ANY)],
            out_specs=pl.BlockSpec((None, LANES, D), tok),
            scratch_shapes=[pltpu.VMEM((2, N_EXPERTS, GATHER_WIN, D), bf16),
                            pltpu.SemaphoreType.DMA((2, N_EXPERTS))]),
        out_shape=jax.ShapeDtypeStruct((B, S, D), f32),
        compiler_params=_cp(("arbitrary",)),
        name="moe_scatter",
    )(base_flat, x3, pos3, sel3, o4)


def _pad_last(a, n):
    return jnp.pad(a, [(0, 0)] * (a.ndim - 1) + [(0, n - a.shape[-1])])


def _rope_tables(S):
    pos = jnp.arange(S, dtype=f32)
    inv_freq = ROPE_THETA ** (-jnp.arange(0, QK_ROPE, 2, dtype=f32) / QK_ROPE)
    ang = pos[:, None] * inv_freq[None, :]
    cos, sin = jnp.cos(ang), jnp.sin(ang)
    half = QK_ROPE // 2
    z_lo = jnp.zeros((S, QK_NOPE), f32)
    z_hi = jnp.zeros((S, LANES - QK_DIM), f32)
    z_half = jnp.zeros((S, half), f32)
    ct = jnp.concatenate([jnp.ones((S, QK_NOPE), f32), cos, cos, z_hi], axis=1)
    s1 = jnp.concatenate([z_lo, -sin, z_half, z_hi], axis=1)
    s2 = jnp.concatenate([z_lo, z_half, sin, z_hi], axis=1)
    return ct, s1, s2


def _hyena_positions(L):
    pos = jnp.arange(L, dtype=f32)
    t = pos / L
    bands = (FILT_EMB - 1) // 2
    f = jnp.linspace(1e-4, bands - 1, bands, dtype=f32)
    w = 2.0 * math.pi * t[:, None] * f[None, :]
    feats = jnp.concatenate([t[:, None], jnp.cos(w), jnp.sin(w)], axis=-1)
    tau = jnp.abs(pos - (L // 2)) / max(L // 2, 1)
    max_decay = math.log(DECAY_TARGET) / FAST_DECAY
    min_decay = math.log(DECAY_TARGET) / SLOW_DECAY
    deltas = jnp.abs(jnp.linspace(min_decay, max_decay, HY_W, dtype=f32))
    planes = lambda t: t.reshape(L // DFT_MINOR, DFT_MINOR, -1).transpose(1, 0, 2).reshape(L, -1)
    return planes(_pad_last(feats, LANES)), planes(tau[:, None]), jnp.tile(deltas, HY_ORDER)[None, :]


def _prep_weights(p):
    L = p["w_in"].shape[0]
    D = p["w_in"].shape[1]
    w_in = p["w_in"]
    kr = jnp.zeros((L, D, LANES), f32).at[:, :, QK_NOPE:QK_DIM].set(w_in[:, :, COL_KROPE:COL_HYENA])
    wq = _pad_last(p["w_q_b"].reshape(L, Q_LORA, N_HEADS, QK_DIM), LANES).reshape(L, Q_LORA, N_HEADS * LANES)
    wkv = p["w_kv_b"].reshape(L, KV_LORA, N_HEADS, QK_NOPE + V_DIM)
    wk = _pad_last(wkv[..., :QK_NOPE], LANES).reshape(L, KV_LORA, N_HEADS * LANES)
    wv = _pad_last(wkv[..., QK_NOPE:], LANES).reshape(L, KV_LORA, N_HEADS * LANES)
    padh = LANES - FILT_HID
    return dict(
        norm1_w=p["norm1_w"][:, None, :],
        wa=jnp.concatenate([w_in[:, :, COL_Q:COL_KROPE], kr], axis=-1).astype(bf16),
        wu=w_in[:, :, COL_HYENA:COL_GATE].astype(bf16),
        wg=w_in[:, :, COL_GATE:].astype(bf16),
        qaw=p["q_a_norm_w"][:, None, :], kvaw=p["kv_a_norm_w"][:, None, :],
        wq=wq.astype(bf16), wk=wk.astype(bf16), wv=wv.astype(bf16),
        qnw=_pad_last(p["q_norm_w"], LANES)[:, None, :], knw=_pad_last(p["k_norm_w"], LANES)[:, None, :],
        scw=p["short_conv_w"], scb=p["short_conv_b"][:, None, :],
        fw1=jnp.pad(p["filt_w1"], ((0, 0), (0, LANES - FILT_EMB), (0, padh))),
        fb1=_pad_last(p["filt_b1"], LANES)[:, None, :], ff1=_pad_last(p["filt_freq1"], LANES)[:, None, :],
        fw2=jnp.pad(p["filt_w2"], ((0, 0), (0, padh), (0, padh))),
        fb2=_pad_last(p["filt_b2"], LANES)[:, None, :], ff2=_pad_last(p["filt_freq2"], LANES)[:, None, :],
        fw3=jnp.pad(p["filt_w3"], ((0, 0), (0, padh), (0, 0))),
        skip=p["filt_skip"],
        w_attn=p["w_attn_branch"].astype(bf16), w_hy=p["w_hyena_branch"].astype(bf16),
        w_out=p["w_out"].astype(bf16),
        norm2_w=p["norm2_w"][:, None, :], wr=_pad_last(p["router_w"], LANES),
        ewg=p["expert_w_gate"].astype(bf16), ewu=p["expert_w_up"].astype(bf16),
        ewd=p["expert_w_down"].astype(bf16),
    )


def _layer(x, w, consts):
    B, S, D = x.shape
    T = B * S
    E = N_EXPERTS
    cap = max(1, EC_FACTOR * S // E)
    nh = S // DFT_MINOR

    a, u, g = _in_proj(x.reshape(T, D), w["norm1_w"], w["wa"], w["wu"], w["wg"])

    q, k, v = _mla_prep(a.reshape(B, S, -1), w["qaw"], w["kvaw"], w["wq"], w["wk"], w["wv"],
                        w["qnw"], w["knw"], consts["ct"], consts["s1"], consts["s2"])
    score_bound = (QK_DIM * (QK_DIM ** -0.5) * math.log2(math.e) * 1.05
                   * jnp.max(jnp.abs(w["qnw"])) * jnp.max(jnp.abs(w["knw"])))
    attn = _flash(q, k, v, score_bound)
    attn = attn.transpose(0, 2, 1, 3).reshape(T, N_HEADS * V_DIM)

    ut = u.reshape(B, nh, DFT_MINOR, -1).transpose(0, 2, 1, 3)
    uct = _short_conv(ut, w["scw"], w["scb"])
    h_un, l1 = _filters(consts["feats"], consts["tau"], w["fw1"], w["fb1"], w["ff1"], w["fw2"], w["fb2"],
                        w["ff2"], w["fw3"], consts["deltas"])
    tb = consts["dft"]
    ha = _dft_a(h_un.reshape(DFT_MINOR, nh, HY_ORDER * HY_W), 0, HY_ORDER * HY_W,
                tb["m1"], tb["twc"], tb["tws"], packed=False)
    hspec = _dft_b_filter(ha, tb["m2f"], 1.0 / l1)
    z4, z_blk = uct, 0
    for n in range(HY_ORDER):
        za = _dft_a(z4, z_blk, HY_W, tb["m1"], tb["twc"], tb["tws"], packed=True)
        zb = _dft_b(za, hspec, n, tb["m2f"], tb["m2i"], tb["twc_t"], tb["tws_t"])
        z4 = _dft_c(zb, tb["gm"], z4, z_blk, uct, n + 1, w["skip"][n:n + 1])
        z_blk = 0
    hy = z4.transpose(0, 2, 1, 3).reshape(T, HY_W)

    x2, hn, aff = _merge(x.reshape(T, D), attn, hy, g, w["w_attn"], w["w_hy"], w["w_out"], w["norm2_w"], w["wr"])

    aff3 = aff.reshape(B, S, LANES)
    aff_t = aff3[:, :, :E].transpose(0, 2, 1)
    thr, need = _thresholds(aff_t.reshape(B * E, S), cap)
    pad_inf = jnp.full((B, LANES - E), jnp.inf, f32)
    thr3 = jnp.concatenate([thr[:, 0].reshape(B, E), pad_inf], axis=1)[:, None, :]
    need3 = _pad_last(need[:, 0].reshape(B, E), LANES)[:, None, :]
    sel, pos, base = _route(aff3, thr3, need3)
    base_flat = base[:, :, :E].transpose(0, 2, 1).reshape(-1).astype(jnp.int32)
    to_rows = lambda t: t[:, :, :E].transpose(0, 2, 1)[:, :, None, :]
    xs, gs = _gather(base_flat, hn.reshape(B, S, D), to_rows(pos), to_rows(sel), aff_t[:, :, None, :], cap)
    o = _ffn(xs, gs, w["ewg"], w["ewu"], w["ewd"])
    return _scatter(base_flat, x2.reshape(B, S, D), pos, sel, o)


def kernel(x, norm1_w, w_in, q_a_norm_w, w_q_b, kv_a_norm_w, w_kv_b, q_norm_w, k_norm_w, short_conv_w,
           short_conv_b, filt_w1, filt_b1, filt_freq1, filt_w2, filt_b2, filt_freq2, filt_w3, filt_skip,
           w_attn_branch, w_hyena_branch, w_out, norm2_w, router_w, expert_w_gate, expert_w_up, expert_w_down):
    B, S, D = x.shape
    assert S % (2 * DFT_MINOR) == 0 and w_hyena_branch.shape[1] == HY_W
    params = dict(norm1_w=norm1_w, w_in=w_in, q_a_norm_w=q_a_norm_w, w_q_b=w_q_b, kv_a_norm_w=kv_a_norm_w,
                  w_kv_b=w_kv_b, q_norm_w=q_norm_w, k_norm_w=k_norm_w, short_conv_w=short_conv_w,
                  short_conv_b=short_conv_b, filt_w1=filt_w1, filt_b1=filt_b1, filt_freq1=filt_freq1,
                  filt_w2=filt_w2, filt_b2=filt_b2, filt_freq2=filt_freq2, filt_w3=filt_w3, filt_skip=filt_skip,
                  w_attn_branch=w_attn_branch, w_hyena_branch=w_hyena_branch, w_out=w_out, norm2_w=norm2_w,
                  router_w=router_w, expert_w_gate=expert_w_gate, expert_w_up=expert_w_up,
                  expert_w_down=expert_w_down)
    weights = _prep_weights(params)
    ct, s1, s2 = _rope_tables(S)
    feats, tau, deltas = _hyena_positions(S)
    consts = dict(ct=ct, s1=s1, s2=s2, feats=feats, tau=tau, deltas=deltas, dft=_dft_tables(S))

    def body(xc, w):
        return _layer(xc, w, consts), None

    out, _ = lax.scan(body, x, weights)
    return out
```

```python
import functools
import math

import jax
import jax.numpy as jnp
from jax import lax
from jax.experimental import pallas as pl
from jax.experimental.pallas import tpu as pltpu

f32 = jnp.float32
bf16 = jnp.bfloat16

N_HEADS = 16
QK_NOPE = 64
QK_ROPE = 32
QK_DIM = QK_NOPE + QK_ROPE
V_DIM = 64
Q_LORA = 384
KV_LORA = 256
ROPE_THETA = 10000.0
HY_W = 1024
HY_ORDER = 2
FILT_EMB = 33
FILT_HID = 64
DECAY_TARGET = 1e-2
FAST_DECAY = 0.3
SLOW_DECAY = 1.5
N_EXPERTS = 16
EC_FACTOR = 2
RMS_EPS = 1e-6

COL_Q = 0
COL_KV = COL_Q + Q_LORA
COL_KROPE = COL_KV + KV_LORA
COL_HYENA = COL_KROPE + QK_ROPE
COL_GATE = COL_HYENA + (HY_ORDER + 1) * HY_W

LANES = 128
SUBLANES = 8
DFT_MINOR = 128
GATHER_WIN = 144
SMALL_WIN = 32
DFT_STEP_PLANES = 2
VT_ROWS = 80
VMEM_LIMIT = 56 * 1024 * 1024


def _cp(sem, vmem=VMEM_LIMIT):
    return pltpu.CompilerParams(dimension_semantics=sem, vmem_limit_bytes=vmem)


def _tile(n, pref):
    t = min(n, pref)
    assert n % t == 0, (n, pref)
    return t


def _dot(a, b):
    return jnp.dot(a, b, preferred_element_type=f32)


def _dot_hi(a, b):
    return jnp.dot(a, b, preferred_element_type=f32, precision=lax.Precision.HIGHEST)


def _in_proj_kernel(x_ref, nw_ref, wa_ref, wu_ref, wg_ref, a_ref, u_ref, g_ref):
    x = x_ref[...]
    ms = jnp.mean(x * x, axis=-1, keepdims=True)
    xn = (x * lax.rsqrt(ms + RMS_EPS) * nw_ref[...]).astype(bf16)
    a_ref[...] = _dot(xn, wa_ref[...])
    u_ref[...] = _dot(xn, wu_ref[...]).astype(bf16)
    g_ref[...] = jax.nn.sigmoid(_dot(xn, wg_ref[...])).astype(bf16)


def _in_proj(x2, nw, wa, wu, wg):
    T, D = x2.shape
    tm = _tile(T, 256)
    na, nu, ng = wa.shape[1], wu.shape[1], wg.shape[1]
    full = lambda i: (0, 0)
    row = lambda i: (i, 0)
    return pl.pallas_call(
        _in_proj_kernel,
        grid=(T // tm,),
        in_specs=[pl.BlockSpec((tm, D), row), pl.BlockSpec((1, D), full),
                  pl.BlockSpec((D, na), full), pl.BlockSpec((D, nu), full), pl.BlockSpec((D, ng), full)],
        out_specs=[pl.BlockSpec((tm, na), row), pl.BlockSpec((tm, nu), row), pl.BlockSpec((tm, ng), row)],
        out_shape=[jax.ShapeDtypeStruct((T, na), f32), jax.ShapeDtypeStruct((T, nu), bf16),
                   jax.ShapeDtypeStruct((T, ng), bf16)],
        compiler_params=_cp(("parallel",)),
        name="in_proj",
    )(x2, nw, wa, wu, wg)


def _head_norm_rope(t, w, ct, s1, s2):
    ss = jnp.sum(t * t, axis=-1, keepdims=True) * (1.0 / QK_DIM)
    t = t * lax.rsqrt(ss + RMS_EPS) * w
    return t * ct + pltpu.roll(t, LANES - QK_ROPE // 2, 1) * s1 + pltpu.roll(t, QK_ROPE // 2, 1) * s2


def _mla_prep_kernel(a_ref, qaw_ref, kvaw_ref, wq_ref, wk_ref, wv_ref, qnw_ref, knw_ref,
                     ct_ref, s1_ref, s2_ref, q_ref, k_ref, v_ref, *, qscale):
    a = a_ref[...]
    ql = a[:, :Q_LORA]
    kvl = a[:, Q_LORA:Q_LORA + KV_LORA]
    kr = a[:, Q_LORA + KV_LORA:]
    qn = (ql * lax.rsqrt(jnp.mean(ql * ql, axis=-1, keepdims=True) + RMS_EPS) * qaw_ref[...]).astype(bf16)
    kvn = (kvl * lax.rsqrt(jnp.mean(kvl * kvl, axis=-1, keepdims=True) + RMS_EPS) * kvaw_ref[...]).astype(bf16)
    q = _dot(qn, wq_ref[...])
    k = _dot(kvn, wk_ref[...])
    v = _dot(kvn, wv_ref[...])
    ct, s1, s2 = ct_ref[...], s1_ref[...], s2_ref[...]
    lane = lax.broadcasted_iota(jnp.int32, (1, LANES), 1)
    ones_col = (lane == V_DIM).astype(f32)
    for h in range(N_HEADS):
        sl = slice(h * LANES, (h + 1) * LANES)
        qh = _head_norm_rope(q[:, sl], qnw_ref[...], ct, s1, s2) * qscale
        kh = _head_norm_rope(k[:, sl] + kr, knw_ref[...], ct, s1, s2)
        q_ref[h] = qh.astype(bf16)
        k_ref[h] = kh.astype(bf16)
        v_ref[h] = (v[:, sl] + ones_col).T[:VT_ROWS].astype(bf16)


def _mla_prep(a3, qaw, kvaw, wq, wk, wv, qnw, knw, ct, s1, s2):
    B, S, NA = a3.shape
    tm = _tile(S, 512)
    HP = N_HEADS * LANES
    qscale = (QK_DIM ** -0.5) * math.log2(math.e)
    full2 = lambda b, i: (0, 0)
    pos = lambda b, i: (i, 0)
    out_spec = pl.BlockSpec((None, N_HEADS, tm, LANES), lambda b, i: (b, 0, i, 0))
    out_sds = jax.ShapeDtypeStruct((B, N_HEADS, S, LANES), bf16)
    return pl.pallas_call(
        functools.partial(_mla_prep_kernel, qscale=qscale),
        grid=(B, S // tm),
        in_specs=[pl.BlockSpec((None, tm, NA), lambda b, i: (b, i, 0)),
                  pl.BlockSpec((1, Q_LORA), full2), pl.BlockSpec((1, KV_LORA), full2),
                  pl.BlockSpec((Q_LORA, HP), full2), pl.BlockSpec((KV_LORA, HP), full2),
                  pl.BlockSpec((KV_LORA, HP), full2),
                  pl.BlockSpec((1, LANES), full2), pl.BlockSpec((1, LANES), full2),
                  pl.BlockSpec((tm, LANES), pos), pl.BlockSpec((tm, LANES), pos), pl.BlockSpec((tm, LANES), pos)],
        out_specs=[out_spec, out_spec,
                   pl.BlockSpec((None, N_HEADS, None, VT_ROWS, tm), lambda b, i: (b, 0, i, 0, 0))],
        out_shape=[out_sds, out_sds, jax.ShapeDtypeStruct((B, N_HEADS, S // tm, VT_ROWS, tm), bf16)],
        compiler_params=_cp(("parallel", "parallel")),
        name="mla_prep",
    )(a3, qaw, kvaw, wq, wk, wv, qnw, knw, ct, s1, s2)


def _flash_kernel(q_ref, k_ref, vt_ref, o_ref, m_sc, acc_sc, s0_sc, s1_sc, *, unroll, online_max):
    nchunk, _, ck = vt_ref.shape
    q = q_ref[...]
    m_sc[...] = jnp.full(m_sc.shape, -jnp.inf, f32)
    acc_sc[...] = jnp.zeros(acc_sc.shape, f32)
    sbuf = (s0_sc, s1_sc)

    def scores(c, buf):
        start = pl.multiple_of(c * ck, ck)
        buf[...] = lax.dot_general(k_ref[pl.ds(start, ck), :], q, (((1,), (1,)), ((), ())),
                                   preferred_element_type=f32)

    def softmax_pv(c, buf):
        st = buf[...]
        if online_max:
            m = m_sc[...]
            m_new = jnp.maximum(m, jnp.max(st, axis=0, keepdims=True))
            pt = jnp.exp2(st - m_new).astype(bf16)
            acc_sc[0:VT_ROWS, :] = jnp.exp2(m - m_new) * acc_sc[0:VT_ROWS, :] + _dot(vt_ref[c], pt)
            m_sc[...] = m_new
        else:
            acc_sc[0:VT_ROWS, :] += _dot(vt_ref[c], jnp.exp2(st).astype(bf16))

    scores(0, s0_sc)

    def body(i, carry):
        for u in range(unroll):
            c = i * unroll + u
            scores(jnp.minimum(c + 1, nchunk - 1), sbuf[(u + 1) % 2])
            softmax_pv(c, sbuf[u % 2])
        return carry

    lax.fori_loop(0, nchunk // unroll, body, 0)
    acc = acc_sc[...].T
    o_ref[...] = (acc[:, :V_DIM] / acc[:, V_DIM:V_DIM + 1]).astype(o_ref.dtype)


SCORE_BOUND_LIMIT = 90.0


def _flash(q, k, vt, score_bound):
    B, H, S, _ = q.shape
    nchunk, ck = vt.shape[2], vt.shape[4]
    tq = _tile(S, 512)
    unroll = 8 if nchunk % 8 == 0 else (4 if nchunk % 4 == 0 else 2)
    assert nchunk % unroll == 0

    def call(online_max):
        return pl.pallas_call(
            functools.partial(_flash_kernel, unroll=unroll, online_max=online_max),
            grid=(B, H, S // tq),
            in_specs=[pl.BlockSpec((None, None, tq, LANES), lambda b, h, i: (b, h, i, 0)),
                      pl.BlockSpec((None, None, S, LANES), lambda b, h, i: (b, h, 0, 0)),
                      pl.BlockSpec((None, None, nchunk, VT_ROWS, ck), lambda b, h, i: (b, h, 0, 0, 0))],
            out_specs=pl.BlockSpec((None, None, tq, V_DIM), lambda b, h, i: (b, h, i, 0)),
            out_shape=jax.ShapeDtypeStruct((B, H, S, V_DIM), bf16),
            scratch_shapes=[pltpu.VMEM((1, tq), f32), pltpu.VMEM((LANES, tq), f32),
                            pltpu.VMEM((ck, tq), f32), pltpu.VMEM((ck, tq), f32)],
            compiler_params=_cp(("parallel", "parallel", "parallel")),
            name="flash_attn_online" if online_max else "flash_attn_bounded")

    return lax.cond(score_bound <= SCORE_BOUND_LIMIT, call(False), call(True), q, k, vt)


def _short_conv_kernel(u_ref, prev_ref, next_ref, w_ref, b_ref, o_ref):
    jb = pl.program_id(1)
    last = pl.num_programs(1) - 1
    nplanes, nh, _ = u_ref.shape
    rows = lax.broadcasted_iota(jnp.int32, (nh, 1), 0)
    prev = prev_ref[...].astype(f32)
    nxt = next_ref[...].astype(f32)
    prev_wrap = jnp.where(rows == 0, 0.0, pltpu.roll(prev, 1, 0))
    next_wrap = jnp.where(rows == nh - 1, 0.0, pltpu.roll(nxt, nh - 1, 0))
    halo_up = jnp.where(jb == 0, prev_wrap, prev)
    halo_dn = jnp.where(jb == last, next_wrap, nxt)
    w = w_ref[...]
    bias = b_ref[...]
    for p in range(nplanes):
        up = halo_up if p == 0 else u_ref[p - 1].astype(f32)
        dn = halo_dn if p == nplanes - 1 else u_ref[p + 1].astype(f32)
        o_ref[p] = (bias + up * w[0:1, :] + u_ref[p].astype(f32) * w[1:2, :] + dn * w[2:3, :]).astype(o_ref.dtype)


def _short_conv(ut, w, b):
    B, P, nh, CT = ut.shape
    tc = _tile(CT, 1024)
    J = 8
    blk = pl.BlockSpec((None, J, nh, tc), lambda bi, j, c: (bi, j, 0, c))
    halo = lambda off: pl.BlockSpec((None, None, nh, tc), lambda bi, j, c: (bi, (j * J + off + P) % P, 0, c))
    return pl.pallas_call(
        _short_conv_kernel,
        grid=(B, P // J, CT // tc),
        in_specs=[blk, halo(-1), halo(J),
                  pl.BlockSpec((3, tc), lambda bi, j, c: (0, c)),
                  pl.BlockSpec((1, tc), lambda bi, j, c: (0, c))],
        out_specs=blk,
        out_shape=jax.ShapeDtypeStruct((B, P, nh, CT), bf16),
        compiler_params=_cp(("parallel", "parallel", "parallel")),
        name="short_conv",
    )(ut, ut, ut, w, b)


def _filter_kernel(feat_ref, tau_ref, w1_ref, b1_ref, f1_ref, w2_ref, b2_ref, f2_ref, w3_ref, dl_ref,
                   h_ref, l1_ref):
    i = pl.program_id(0)
    h = jnp.sin(f1_ref[...] * (_dot_hi(feat_ref[...], w1_ref[...]) + b1_ref[...]))
    h = jnp.sin(f2_ref[...] * (_dot_hi(h, w2_ref[...]) + b2_ref[...]))
    h = _dot_hi(h, w3_ref[...])
    h = h * jnp.exp(-tau_ref[...] * dl_ref[...])
    h_ref[...] = h

    @pl.when(i == 0)
    def _():
        l1_ref[...] = jnp.zeros(l1_ref.shape, f32)

    l1_ref[...] += jnp.sum(jnp.abs(h), axis=0, keepdims=True)


def _filters(feats, tau, w1, b1, f1, w2, b2, f2, w3, dl):
    S = feats.shape[0]
    CO = w3.shape[1]
    tm = _tile(S, 512)
    full = lambda i: (0, 0)
    row = lambda i: (i, 0)
    return pl.pallas_call(
        _filter_kernel,
        grid=(S // tm,),
        in_specs=[pl.BlockSpec((tm, LANES), row), pl.BlockSpec((tm, 1), row),
                  pl.BlockSpec((LANES, LANES), full), pl.BlockSpec((1, LANES), full), pl.BlockSpec((1, LANES), full),
                  pl.BlockSpec((LANES, LANES), full), pl.BlockSpec((1, LANES), full), pl.BlockSpec((1, LANES), full),
                  pl.BlockSpec((LANES, CO), full), pl.BlockSpec((1, CO), full)],
        out_specs=[pl.BlockSpec((tm, CO), row), pl.BlockSpec((1, CO), full)],
        out_shape=[jax.ShapeDtypeStruct((S, CO), f32), jax.ShapeDtypeStruct((1, CO), f32)],
        compiler_params=_cp(("arbitrary",)),
        name="hyena_filters",
    )(feats, tau, w1, b1, f1, w2, b2, f2, w3, dl)


def _pack_pair(re, im):
    hi = pltpu.bitcast(re.astype(bf16).astype(f32), jnp.uint32) & jnp.uint32(0xFFFF0000)
    lo = pltpu.bitcast(im.astype(bf16).astype(f32), jnp.uint32) >> 16
    return hi | lo


def _unpack_pair(w):
    re = pltpu.bitcast(w & jnp.uint32(0xFFFF0000), f32)
    im = pltpu.bitcast(w << 16, f32)
    return re.astype(bf16), im.astype(bf16)


def _dft_a_kernel(x_ref, m_ref, tc_ref, ts_ref, a_ref, *, packed):
    n1f = m_ref.shape[0] // 2
    m = m_ref[...]
    for j in range(tc_ref.shape[0]):
        if packed:
            pr = _dot(m, x_ref[0, j].astype(bf16))
            pi = _dot(m, x_ref[1, j].astype(bf16))
            ar = pr[:n1f] + pi[n1f:]
            ai = pi[:n1f] - pr[n1f:]
        else:
            p = _dot(m, x_ref[j].astype(bf16))
            ar = p[:n1f]
            ai = -p[n1f:]
        tc, ts = tc_ref[j], ts_ref[j]
        a_ref[:, j, :] = _pack_pair(ar * tc + ai * ts, ai * tc - ar * ts)


def _dft_a(x4, col_blk, C, m1, twc, tws, packed):
    n1f = m1.shape[0] // 2
    nh = m1.shape[1]
    J = SUBLANES
    cb = _tile(C, 1024)
    if packed:
        x_spec = pl.BlockSpec((2, J, nh, cb), lambda j, c: (0, j, 0, col_blk + c))
    else:
        x_spec = pl.BlockSpec((J, nh, cb), lambda j, c: (j, 0, col_blk + c))
    return pl.pallas_call(
        functools.partial(_dft_a_kernel, packed=packed),
        grid=(DFT_MINOR // J, C // cb),
        in_specs=[x_spec,
                  pl.BlockSpec((2 * n1f, nh), lambda j, c: (0, 0)),
                  pl.BlockSpec((J, n1f, 1), lambda j, c: (j, 0, 0)),
                  pl.BlockSpec((J, n1f, 1), lambda j, c: (j, 0, 0))],
        out_specs=pl.BlockSpec((n1f, J, cb), lambda j, c: (0, j, c)),
        out_shape=jax.ShapeDtypeStruct((n1f, DFT_MINOR, C), jnp.uint32),
        compiler_params=_cp(("parallel", "parallel")),
        name="dft_stage_a",
    )(x4, m1, twc, tws)


def _dft_bf_kernel(a_ref, m2_ref, inv_ref, h_ref):
    C = a_ref.shape[-1]
    for j in range(a_ref.shape[0]):
        a = jnp.concatenate(_unpack_pair(a_ref[j]), axis=0)
        z = _dot(m2_ref[...], a) * inv_ref[...]
        h_ref[j] = z.reshape(2, DFT_MINOR, C).astype(h_ref.dtype)


def _dft_b_filter(a3, m2f, inv_l1):
    n1f, _, C = a3.shape
    J = DFT_STEP_PLANES
    return pl.pallas_call(
        _dft_bf_kernel,
        grid=(n1f // J,),
        in_specs=[pl.BlockSpec((J, DFT_MINOR, C), lambda k: (k, 0, 0)),
                  pl.BlockSpec((2 * DFT_MINOR, 2 * DFT_MINOR), lambda k: (0, 0)),
                  pl.BlockSpec((1, C), lambda k: (0, 0))],
        out_specs=pl.BlockSpec((J, 2, DFT_MINOR, C), lambda k: (k, 0, 0, 0)),
        out_shape=jax.ShapeDtypeStruct((n1f, 2, DFT_MINOR, C), bf16),
        compiler_params=_cp(("parallel",)),
        name="dft_stage_b_filter",
    )(a3, m2f, inv_l1)


def _dft_b_kernel(a_ref, h_ref, m2f_ref, m2i_ref, tc_ref, ts_ref, b_ref):
    for j in range(a_ref.shape[0]):
        a = jnp.concatenate(_unpack_pair(a_ref[j]), axis=0)
        z = _dot(m2f_ref[...], a)
        zr, zi = z[:DFT_MINOR], z[DFT_MINOR:]
        hr, hi = h_ref[j, 0].astype(f32), h_ref[j, 1].astype(f32)
        y = jnp.concatenate([zr * hr - zi * hi, zr * hi + zi * hr], axis=0).astype(bf16)
        w = _dot(m2i_ref[...], y)
        wr, wi = w[:DFT_MINOR], w[DFT_MINOR:]
        tc, ts = tc_ref[j], ts_ref[j]
        b_ref[:, j, :] = _pack_pair(wr * tc - wi * ts, wr * ts + wi * tc)


def _dft_b(a3, hspec, order, m2f, m2i, twc_t, tws_t):
    n1f, _, C = a3.shape
    J = SUBLANES
    return pl.pallas_call(
        _dft_b_kernel,
        grid=(n1f // J,),
        in_specs=[pl.BlockSpec((J, DFT_MINOR, C), lambda k: (k, 0, 0)),
                  pl.BlockSpec((J, 2, DFT_MINOR, C), lambda k: (k, 0, 0, order)),
                  pl.BlockSpec((2 * DFT_MINOR, 2 * DFT_MINOR), lambda k: (0, 0)),
                  pl.BlockSpec((2 * DFT_MINOR, 2 * DFT_MINOR), lambda k: (0, 0)),
                  pl.BlockSpec((J, DFT_MINOR, 1), lambda k: (k, 0, 0)),
                  pl.BlockSpec((J, DFT_MINOR, 1), lambda k: (k, 0, 0))],
        out_specs=pl.BlockSpec((DFT_MINOR, J, C), lambda k: (0, k, 0)),
        out_shape=jax.ShapeDtypeStruct((DFT_MINOR, n1f, C), jnp.uint32),
        compiler_params=_cp(("parallel",)),
        name="dft_stage_b",
    )(a3, hspec, m2f, m2i, twc_t, tws_t)


def _dft_c_kernel(b_ref, g_ref, z_ref, xg_ref, skip_ref, o_ref):
    n1f = b_ref.shape[1]
    nh = n1f // 2
    skip = skip_ref[...]
    for j in range(b_ref.shape[0]):
        bb = jnp.concatenate(_unpack_pair(b_ref[j]), axis=0)
        y = _dot(g_ref[...], bb)
        for bi in range(2):
            z = z_ref[bi, j].astype(f32)
            o_ref[bi, j] = (xg_ref[bi, j].astype(f32) * (y[bi * nh:(bi + 1) * nh] + z * skip)).astype(o_ref.dtype)


def _dft_c(b3, gm, z4, z_blk, xg4, xg_blk, skip):
    _, n1f, C = b3.shape
    nh = n1f // 2
    J = DFT_STEP_PLANES
    return pl.pallas_call(
        _dft_c_kernel,
        grid=(DFT_MINOR // J,),
        in_specs=[pl.BlockSpec((J, n1f, C), lambda j: (j, 0, 0)),
                  pl.BlockSpec((n1f, 2 * n1f), lambda j: (0, 0)),
                  pl.BlockSpec((2, J, nh, C), lambda j: (0, j, 0, z_blk)),
                  pl.BlockSpec((2, J, nh, C), lambda j: (0, j, 0, xg_blk)),
                  pl.BlockSpec((1, C), lambda j: (0, 0))],
        out_specs=pl.BlockSpec((2, J, nh, C), lambda j: (0, j, 0, 0)),
        out_shape=jax.ShapeDtypeStruct((2, DFT_MINOR, nh, C), bf16),
        compiler_params=_cp(("parallel",)),
        name="dft_stage_c",
    )(b3, gm, z4, xg4, skip)


def _dft_tables(S):
    N = 2 * S
    n1f = N // DFT_MINOR
    nh = n1f // 2
    two_pi = 2.0 * math.pi

    def cs(num, den):
        ang = (num % den).astype(f32) * (two_pi / den)
        return jnp.cos(ang), jnp.sin(ang)

    k1 = jnp.arange(n1f, dtype=jnp.int32)
    n1 = jnp.arange(nh, dtype=jnp.int32)
    c1, s1 = cs(k1[:, None] * n1[None, :], n1f)
    m1 = jnp.concatenate([c1, s1], axis=0).astype(bf16)
    n2 = jnp.arange(DFT_MINOR, dtype=jnp.int32)
    twc, tws = cs(n2[:, None] * k1[None, :], N)
    c2, s2 = cs(n2[:, None] * n2[None, :], DFT_MINOR)
    m2f = jnp.concatenate([jnp.concatenate([c2, s2], 1), jnp.concatenate([-s2, c2], 1)], 0).astype(bf16)
    m2i = jnp.concatenate([jnp.concatenate([c2, -s2], 1), jnp.concatenate([s2, c2], 1)], 0).astype(bf16)
    rows = jnp.arange(nh, dtype=jnp.int32) + n1f // 4
    gc, gs = cs(rows[:, None] * k1[None, :], n1f)
    gm = (jnp.concatenate([jnp.concatenate([gc, -gs], 1), jnp.concatenate([gs, gc], 1)], 0) / N).astype(bf16)
    return dict(m1=m1, twc=twc[:, :, None], tws=tws[:, :, None],
                twc_t=twc.T[:, :, None], tws_t=tws.T[:, :, None], m2f=m2f, m2i=m2i, gm=gm)


def _merge_kernel(x_ref, at_ref, hy_ref, g_ref, wa_ref, wh_ref, wo_ref, n2_ref, wr_ref,
                  xo_ref, hn_ref, aff_ref):
    D = x_ref.shape[1]
    g = g_ref[...].astype(f32)
    merged = g[:, :D] * _dot(at_ref[...], wa_ref[...]) + g[:, D:] * _dot(hy_ref[...], wh_ref[...])
    x = x_ref[...] + _dot(merged.astype(bf16), wo_ref[...])
    xo_ref[...] = x
    hn = x * lax.rsqrt(jnp.mean(x * x, axis=-1, keepdims=True) + RMS_EPS) * n2_ref[...]
    hn_ref[...] = hn.astype(bf16)
    logits = _dot_hi(hn, wr_ref[...])
    lane = lax.broadcasted_iota(jnp.int32, logits.shape, 1)
    logits = jnp.where(lane < N_EXPERTS, logits, -jnp.inf)
    e = jnp.exp(logits - jnp.max(logits, axis=-1, keepdims=True))
    aff_ref[...] = e / jnp.sum(e, axis=-1, keepdims=True)


def _merge(x2, attn, hy, g, wa, wh, wo, n2w, wr):
    T, D = x2.shape
    tm = _tile(T, 512)
    row = lambda i: (i, 0)
    full = lambda i: (0, 0)
    return pl.pallas_call(
        _merge_kernel,
        grid=(T // tm,),
        in_specs=[pl.BlockSpec((tm, D), row), pl.BlockSpec((tm, D), row), pl.BlockSpec((tm, D), row),
                  pl.BlockSpec((tm, 2 * D), row),
                  pl.BlockSpec((D, D), full), pl.BlockSpec((D, D), full), pl.BlockSpec((D, D), full),
                  pl.BlockSpec((1, D), full), pl.BlockSpec((D, LANES), full)],
        out_specs=[pl.BlockSpec((tm, D), row), pl.BlockSpec((tm, D), row), pl.BlockSpec((tm, LANES), row)],
        out_shape=[jax.ShapeDtypeStruct((T, D), f32), jax.ShapeDtypeStruct((T, D), bf16),
                   jax.ShapeDtypeStruct((T, LANES), f32)],
        compiler_params=_cp(("parallel",)),
        name="merge_out_router",
    )(x2, attn, hy, g, wa, wh, wo, n2w, wr)


def _thresh_kernel(a_ref, thr_ref, need_ref, *, cap):
    bits = pltpu.bitcast(a_ref[...], jnp.int32)
    R = bits.shape[0]

    def body(i, thr):
        cand = thr | jnp.left_shift(jnp.int32(1), 30 - i)
        cnt = jnp.sum((bits >= cand).astype(f32), axis=1, keepdims=True)
        return jnp.where(cnt >= cap, cand, thr)

    thr = lax.fori_loop(0, 31, body, jnp.zeros((R, 1), jnp.int32))
    n_gt = jnp.sum((bits > thr).astype(f32), axis=1, keepdims=True)
    thr_ref[...] = jnp.broadcast_to(pltpu.bitcast(thr, f32), thr_ref.shape)
    need_ref[...] = jnp.broadcast_to(cap - n_gt, need_ref.shape)


def _thresholds(aff_t, cap):
    R, S = aff_t.shape
    return pl.pallas_call(
        functools.partial(_thresh_kernel, cap=float(cap)),
        out_shape=[jax.ShapeDtypeStruct((R, LANES), f32), jax.ShapeDtypeStruct((R, LANES), f32)],
        compiler_params=_cp(None),
        name="route_threshold",
    )(aff_t)


def _route_kernel(aff_ref, thr_ref, need_ref, ltri_ref, slab_ref, sel_ref, pos_ref, base_ref, ceq_sc, csel_sc):
    k = pl.program_id(1)

    @pl.when(k == 0)
    def _():
        ceq_sc[...] = jnp.zeros(ceq_sc.shape, f32)
        csel_sc[...] = jnp.zeros(csel_sc.shape, f32)

    a = aff_ref[...]
    thr = thr_ref[...]
    gt = a > thr
    eq = a == thr
    eqf = eq.astype(f32)
    ltri = ltri_ref[...]
    eq_rank = ceq_sc[...] + _dot(ltri, eqf.astype(bf16))
    sel = jnp.logical_or(gt, jnp.logical_and(eq, eq_rank < need_ref[...]))
    self_ = sel.astype(f32)
    pos = csel_sc[...] + _dot(ltri, self_.astype(bf16))
    sel_ref[...] = self_
    pos_ref[...] = pos
    base_ref[...] = csel_sc[...] + _dot(slab_ref[...], self_.astype(bf16))
    ceq_sc[...] += jnp.sum(eqf, axis=0, keepdims=True)
    csel_sc[...] += jnp.sum(self_, axis=0, keepdims=True)


def _route(aff3, thr3, need3):
    B, S, _ = aff3.shape
    tk = _tile(S, 1024)
    ltri = (jnp.arange(tk)[:, None] > jnp.arange(tk)[None, :]).astype(bf16)
    tok = pl.BlockSpec((None, tk, LANES), lambda b, k: (b, k, 0))
    per_b = pl.BlockSpec((None, 1, LANES), lambda b, k: (b, 0, 0))
    return pl.pallas_call(
        _route_kernel,
        grid=(B, S // tk),
        in_specs=[tok, per_b, per_b, pl.BlockSpec((tk, tk), lambda b, k: (0, 0)),
                  pl.BlockSpec((tk // LANES, tk), lambda b, k: (0, 0))],
        out_specs=[tok, tok, pl.BlockSpec((None, tk // LANES, LANES), lambda b, k: (b, k, 0))],
        out_shape=[jax.ShapeDtypeStruct((B, S, LANES), f32), jax.ShapeDtypeStruct((B, S, LANES), f32),
                   jax.ShapeDtypeStruct((B, S // LANES, LANES), f32)],
        scratch_shapes=[pltpu.VMEM((1, LANES), f32), pltpu.VMEM((1, LANES), f32)],
        compiler_params=_cp(("parallel", "arbitrary")),
        name="route_positions",
    )(aff3, thr3, need3, ltri, ltri[::LANES])


def _gather_kernel(base_sm, cnt_sm, hn_ref, pos_ref, sel_ref, gate_ref, xs_ref, gs_ref, acc_sc, gacc_sc, *,
                   nsub, nsub_tot):
    b, e, k = pl.program_id(0), pl.program_id(1), pl.program_id(2)
    cap = xs_ref.shape[0]

    @pl.when(k == 0)
    def _():
        acc_sc[...] = jnp.zeros(acc_sc.shape, f32)
        gacc_sc[...] = jnp.zeros(gacc_sc.shape, f32)

    for j in range(nsub):
        idx = (b * N_EXPERTS + e) * nsub_tot + k * nsub + j
        base = base_sm[idx]
        start = pl.multiple_of((base // SUBLANES) * SUBLANES, SUBLANES)
        sl = slice(j * LANES, (j + 1) * LANES)

        def place(win):
            rows = lax.broadcasted_iota(jnp.int32, (win, LANES), 0).astype(f32)
            rel = pos_ref[:, sl] - start.astype(f32)
            hit = jnp.logical_and(rows == rel, sel_ref[:, sl] > 0.0)
            oht = jnp.where(hit, 1.0, 0.0)
            acc_sc[pl.ds(start, win), :] += _dot(oht.astype(bf16), hn_ref[sl, :])
            gacc_sc[pl.ds(start, win), :] += jnp.sum(oht * gate_ref[:, sl], axis=1, keepdims=True)

        small = cnt_sm[idx] <= SMALL_WIN - SUBLANES
        pl.when(small)(functools.partial(place, SMALL_WIN))
        pl.when(jnp.logical_not(small))(functools.partial(place, GATHER_WIN))

    @pl.when(k == pl.num_programs(2) - 1)
    def _():
        xs_ref[...] = acc_sc[0:cap, :].astype(xs_ref.dtype)
        gs_ref[...] = gacc_sc[0:cap, :]


def _gather(base_flat, cnt_flat, hn3, pos_t, sel_t, gate_t, cap):
    B, S, D = hn3.shape
    tk = _tile(S, 1024)
    nsub = tk // LANES
    row = pl.BlockSpec((None, None, 1, tk), lambda b, e, k, base, cnt: (b, e, 0, k))
    return pl.pallas_call(
        functools.partial(_gather_kernel, nsub=nsub, nsub_tot=S // LANES),
        grid_spec=pltpu.PrefetchScalarGridSpec(
            num_scalar_prefetch=2,
            grid=(B, N_EXPERTS, S // tk),
            in_specs=[pl.BlockSpec((None, tk, D), lambda b, e, k, base, cnt: (b, k, 0)), row, row, row],
            out_specs=[pl.BlockSpec((None, None, cap, D), lambda b, e, k, base, cnt: (b, e, 0, 0)),
                       pl.BlockSpec((None, None, cap, 1), lambda b, e, k, base, cnt: (b, e, 0, 0))],
            scratch_shapes=[pltpu.VMEM((cap + GATHER_WIN, D), f32), pltpu.VMEM((cap + GATHER_WIN, 1), f32)]),
        out_shape=[jax.ShapeDtypeStruct((B, N_EXPERTS, cap, D), bf16),
                   jax.ShapeDtypeStruct((B, N_EXPERTS, cap, 1), f32)],
        compiler_params=_cp(("parallel", "parallel", "arbitrary")),
        name="moe_gather",
    )(base_flat, cnt_flat, hn3, pos_t, sel_t, gate_t)


def _ffn_kernel(xs_ref, gs_ref, wg_ref, wu_ref, wd_ref, o_ref):
    xs = xs_ref[...]
    a = _dot(xs, wg_ref[...])
    u = _dot(xs, wu_ref[...])
    hmid = (a * jax.nn.sigmoid(a) * u).astype(bf16)
    o_ref[...] = (_dot(hmid, wd_ref[...]) * gs_ref[...]).astype(o_ref.dtype)


def _ffn(xs, gs, wg, wu, wd):
    B, E, cap, D = xs.shape
    F = wg.shape[-1]
    tm = _tile(cap, 512)
    return pl.pallas_call(
        _ffn_kernel,
        grid=(E, B, cap // tm),
        in_specs=[pl.BlockSpec((None, None, tm, D), lambda e, b, i: (b, e, i, 0)),
                  pl.BlockSpec((None, None, tm, 1), lambda e, b, i: (b, e, i, 0)),
                  pl.BlockSpec((None, D, F), lambda e, b, i: (e, 0, 0)),
                  pl.BlockSpec((None, D, F), lambda e, b, i: (e, 0, 0)),
                  pl.BlockSpec((None, F, D), lambda e, b, i: (e, 0, 0))],
        out_specs=pl.BlockSpec((None, None, tm, D), lambda e, b, i: (b, e, i, 0)),
        out_shape=jax.ShapeDtypeStruct((B, E, cap, D), bf16),
        compiler_params=_cp(("parallel", "parallel", "parallel")),
        name="expert_ffn",
    )(xs, gs, wg, wu, wd)


def _scatter_kernel(base_sm, cnt_sm, x_ref, pos_ref, sel_ref, o_hbm, out_ref, buf, sem, *, nsub_tot, cap):
    g = pl.program_id(0)

    def window(step, e, win):
        b, k = step // nsub_tot, step % nsub_tot
        base = base_sm[(b * N_EXPERTS + e) * nsub_tot + k]
        start = jnp.minimum((base // SUBLANES) * SUBLANES, cap - win)
        start = pl.multiple_of(start, SUBLANES)
        slot = step % 2
        return start, pltpu.make_async_copy(o_hbm.at[b, e, pl.ds(start, win), :],
                                            buf.at[slot, e, pl.ds(0, win), :], sem.at[slot, e])

    def is_small(step, e):
        b, k = step // nsub_tot, step % nsub_tot
        return cnt_sm[(b * N_EXPERTS + e) * nsub_tot + k] <= SMALL_WIN - SUBLANES

    def fetch(step):
        for e in range(N_EXPERTS):
            small = is_small(step, e)
            pl.when(small)(lambda: window(step, e, SMALL_WIN)[1].start())
            pl.when(jnp.logical_not(small))(lambda: window(step, e, GATHER_WIN)[1].start())

    @pl.when(g == 0)
    def _():
        fetch(g)

    @pl.when(g + 1 < pl.num_programs(0))
    def _():
        fetch(g + 1)

    acc = x_ref[...]
    pos = pos_ref[...]
    sel = sel_ref[...]
    for e in range(N_EXPERTS):

        def apply(win):
            start, cp = window(g, e, win)
            cp.wait()
            lanes = lax.broadcasted_iota(jnp.int32, (LANES, win), 1).astype(f32)
            rel = pos[:, e:e + 1] - start.astype(f32)
            hit = jnp.logical_and(lanes == rel, sel[:, e:e + 1] > 0.0)
            return _dot(jnp.where(hit, 1.0, 0.0).astype(bf16), buf[g % 2, e, 0:win, :])

        acc = acc + lax.cond(is_small(g, e), functools.partial(apply, SMALL_WIN),
                             functools.partial(apply, GATHER_WIN))
    out_ref[...] = acc


def _scatter(base_flat, cnt_flat, x3, pos3, sel3, o4):
    B, S, D = x3.shape
    cap = o4.shape[2]
    nsub = S // LANES
    assert cap >= GATHER_WIN
    tok = lambda g, base, cnt: (g // nsub, g % nsub, 0)
    return pl.pallas_call(
        functools.partial(_scatter_kernel, nsub_tot=nsub, cap=cap),
        grid_spec=pltpu.PrefetchScalarGridSpec(
            num_scalar_prefetch=2,
            grid=(B * nsub,),
            in_specs=[pl.BlockSpec((None, LANES, D), tok), pl.BlockSpec((None, LANES, LANES), tok),
                      pl.BlockSpec((None, LANES, LANES), tok), pl.BlockSpec(memory_space=pl.ANY)],
            out_specs=pl.BlockSpec((None, LANES, D), tok),
            scratch_shapes=[pltpu.VMEM((2, N_EXPERTS, GATHER_WIN, D), bf16),
                            pltpu.SemaphoreType.DMA((2, N_EXPERTS))]),
        out_shape=jax.ShapeDtypeStruct((B, S, D), f32),
        compiler_params=_cp(("arbitrary",)),
        name="moe_scatter",
    )(base_flat, cnt_flat, x3, pos3, sel3, o4)


def _pad_last(a, n):
    return jnp.pad(a, [(0, 0)] * (a.ndim - 1) + [(0, n - a.shape[-1])])


def _rope_tables(S):
    pos = jnp.arange(S, dtype=f32)
    inv_freq = ROPE_THETA ** (-jnp.arange(0, QK_ROPE, 2, dtype=f32) / QK_ROPE)
    ang = pos[:, None] * inv_freq[None, :]
    cos, sin = jnp.cos(ang), jnp.sin(ang)
    half = QK_ROPE // 2
    z_lo = jnp.zeros((S, QK_NOPE), f32)
    z_hi = jnp.zeros((S, LANES - QK_DIM), f32)
    z_half = jnp.zeros((S, half), f32)
    ct = jnp.concatenate([jnp.ones((S, QK_NOPE), f32), cos, cos, z_hi], axis=1)
    s1 = jnp.concatenate([z_lo, -sin, z_half, z_hi], axis=1)
    s2 = jnp.concatenate([z_lo, z_half, sin, z_hi], axis=1)
    return ct, s1, s2


def _hyena_positions(L):
    pos = jnp.arange(L, dtype=f32)
    t = pos / L
    bands = (FILT_EMB - 1) // 2
    f = jnp.linspace(1e-4, bands - 1, bands, dtype=f32)
    w = 2.0 * math.pi * t[:, None] * f[None, :]
    feats = jnp.concatenate([t[:, None], jnp.cos(w), jnp.sin(w)], axis=-1)
    tau = jnp.abs(pos - (L // 2)) / max(L // 2, 1)
    max_decay = math.log(DECAY_TARGET) / FAST_DECAY
    min_decay = math.log(DECAY_TARGET) / SLOW_DECAY
    deltas = jnp.abs(jnp.linspace(min_decay, max_decay, HY_W, dtype=f32))
    planes = lambda t: t.reshape(L // DFT_MINOR, DFT_MINOR, -1).transpose(1, 0, 2).reshape(L, -1)
    return planes(_pad_last(feats, LANES)), planes(tau[:, None]), jnp.tile(deltas, HY_ORDER)[None, :]


def _prep_weights(p):
    L = p["w_in"].shape[0]
    D = p["w_in"].shape[1]
    w_in = p["w_in"]
    kr = jnp.zeros((L, D, LANES), f32).at[:, :, QK_NOPE:QK_DIM].set(w_in[:, :, COL_KROPE:COL_HYENA])
    wq = _pad_last(p["w_q_b"].reshape(L, Q_LORA, N_HEADS, QK_DIM), LANES).reshape(L, Q_LORA, N_HEADS * LANES)
    wkv = p["w_kv_b"].reshape(L, KV_LORA, N_HEADS, QK_NOPE + V_DIM)
    wk = _pad_last(wkv[..., :QK_NOPE], LANES).reshape(L, KV_LORA, N_HEADS * LANES)
    wv = _pad_last(wkv[..., QK_NOPE:], LANES).reshape(L, KV_LORA, N_HEADS * LANES)
    padh = LANES - FILT_HID
    return dict(
        norm1_w=p["norm1_w"][:, None, :],
        wa=jnp.concatenate([w_in[:, :, COL_Q:COL_KROPE], kr], axis=-1).astype(bf16),
        wu=w_in[:, :, COL_HYENA:COL_GATE].astype(bf16),
        wg=w_in[:, :, COL_GATE:].astype(bf16),
        qaw=p["q_a_norm_w"][:, None, :], kvaw=p["kv_a_norm_w"][:, None, :],
        wq=wq.astype(bf16), wk=wk.astype(bf16), wv=wv.astype(bf16),
        qnw=_pad_last(p["q_norm_w"], LANES)[:, None, :], knw=_pad_last(p["k_norm_w"], LANES)[:, None, :],
        scw=p["short_conv_w"], scb=p["short_conv_b"][:, None, :],
        fw1=jnp.pad(p["filt_w1"], ((0, 0), (0, LANES - FILT_EMB), (0, padh))),
        fb1=_pad_last(p["filt_b1"], LANES)[:, None, :], ff1=_pad_last(p["filt_freq1"], LANES)[:, None, :],
        fw2=jnp.pad(p["filt_w2"], ((0, 0), (0, padh), (0, padh))),
        fb2=_pad_last(p["filt_b2"], LANES)[:, None, :], ff2=_pad_last(p["filt_freq2"], LANES)[:, None, :],
        fw3=jnp.pad(p["filt_w3"], ((0, 0), (0, padh), (0, 0))),
        skip=p["filt_skip"],
        w_attn=p["w_attn_branch"].astype(bf16), w_hy=p["w_hyena_branch"].astype(bf16),
        w_out=p["w_out"].astype(bf16),
        norm2_w=p["norm2_w"][:, None, :], wr=_pad_last(p["router_w"], LANES),
        ewg=p["expert_w_gate"].astype(bf16), ewu=p["expert_w_up"].astype(bf16),
        ewd=p["expert_w_down"].astype(bf16),
    )


def _layer(x, w, consts):
    B, S, D = x.shape
    T = B * S
    E = N_EXPERTS
    cap = max(1, EC_FACTOR * S // E)
    nh = S // DFT_MINOR

    a, u, g = _in_proj(x.reshape(T, D), w["norm1_w"], w["wa"], w["wu"], w["wg"])

    q, k, v = _mla_prep(a.reshape(B, S, -1), w["qaw"], w["kvaw"], w["wq"], w["wk"], w["wv"],
                        w["qnw"], w["knw"], consts["ct"], consts["s1"], consts["s2"])
    score_bound = (QK_DIM * (QK_DIM ** -0.5) * math.log2(math.e) * 1.05
                   * jnp.max(jnp.abs(w["qnw"])) * jnp.max(jnp.abs(w["knw"])))
    attn = _flash(q, k, v, score_bound)
    attn = attn.transpose(0, 2, 1, 3).reshape(T, N_HEADS * V_DIM)

    ut = u.reshape(B, nh, DFT_MINOR, -1).transpose(0, 2, 1, 3)
    uct = _short_conv(ut, w["scw"], w["scb"])
    h_un, l1 = _filters(consts["feats"], consts["tau"], w["fw1"], w["fb1"], w["ff1"], w["fw2"], w["fb2"],
                        w["ff2"], w["fw3"], consts["deltas"])
    tb = consts["dft"]
    ha = _dft_a(h_un.reshape(DFT_MINOR, nh, HY_ORDER * HY_W), 0, HY_ORDER * HY_W,
                tb["m1"], tb["twc"], tb["tws"], packed=False)
    hspec = _dft_b_filter(ha, tb["m2f"], 1.0 / l1)
    z4, z_blk = uct, 0
    for n in range(HY_ORDER):
        za = _dft_a(z4, z_blk, HY_W, tb["m1"], tb["twc"], tb["tws"], packed=True)
        zb = _dft_b(za, hspec, n, tb["m2f"], tb["m2i"], tb["twc_t"], tb["tws_t"])
        z4 = _dft_c(zb, tb["gm"], z4, z_blk, uct, n + 1, w["skip"][n:n + 1])
        z_blk = 0
    hy = z4.transpose(0, 2, 1, 3).reshape(T, HY_W)

    x2, hn, aff = _merge(x.reshape(T, D), attn, hy, g, w["w_attn"], w["w_hy"], w["w_out"], w["norm2_w"], w["wr"])

    aff3 = aff.reshape(B, S, LANES)
    aff_t = aff3[:, :, :E].transpose(0, 2, 1)
    thr, need = _thresholds(aff_t.reshape(B * E, S), cap)
    pad_inf = jnp.full((B, LANES - E), jnp.inf, f32)
    thr3 = jnp.concatenate([thr[:, 0].reshape(B, E), pad_inf], axis=1)[:, None, :]
    need3 = _pad_last(need[:, 0].reshape(B, E), LANES)[:, None, :]
    sel, pos, base = _route(aff3, thr3, need3)
    base_be = base[:, :, :E].transpose(0, 2, 1).astype(jnp.int32)
    cnt_flat = jnp.diff(base_be, axis=-1, append=cap).reshape(-1)
    base_flat = base_be.reshape(-1)
    to_rows = lambda t: t[:, :, :E].transpose(0, 2, 1)[:, :, None, :]
    xs, gs = _gather(base_flat, cnt_flat, hn.reshape(B, S, D), to_rows(pos), to_rows(sel), aff_t[:, :, None, :], cap)
    o = _ffn(xs, gs, w["ewg"], w["ewu"], w["ewd"])
    return _scatter(base_flat, cnt_flat, x2.reshape(B, S, D), pos, sel, o)


def kernel(x, norm1_w, w_in, q_a_norm_w, w_q_b, kv_a_norm_w, w_kv_b, q_norm_w, k_norm_w, short_conv_w,
           short_conv_b, filt_w1, filt_b1, filt_freq1, filt_w2, filt_b2, filt_freq2, filt_w3, filt_skip,
           w_attn_branch, w_hyena_branch, w_out, norm2_w, router_w, expert_w_gate, expert_w_up, expert_w_down):
    B, S, D = x.shape
    assert S % (2 * DFT_MINOR) == 0 and w_hyena_branch.shape[1] == HY_W
    params = dict(norm1_w=norm1_w, w_in=w_in, q_a_norm_w=q_a_norm_w, w_q_b=w_q_b, kv_a_norm_w=kv_a_norm_w,
                  w_kv_b=w_kv_b, q_norm_w=q_norm_w, k_norm_w=k_norm_w, short_conv_w=short_conv_w,
                  short_conv_b=short_conv_b, filt_w1=filt_w1, filt_b1=filt_b1, filt_freq1=filt_freq1,
                  filt_w2=filt_w2, filt_b2=filt_b2, filt_freq2=filt_freq2, filt_w3=filt_w3, filt_skip=filt_skip,
                  w_attn_branch=w_attn_branch, w_hyena_branch=w_hyena_branch, w_out=w_out, norm2_w=norm2_w,
                  router_w=router_w, expert_w_gate=expert_w_gate, expert_w_up=expert_w_up,
                  expert_w_down=expert_w_down)
    weights = _prep_weights(params)
    ct, s1, s2 = _rope_tables(S)
    feats, tau, deltas = _hyena_positions(S)
    consts = dict(ct=ct, s1=s1, s2=s2, feats=feats, tau=tau, deltas=deltas, dft=_dft_tables(S))

    def body(xc, w):
        return _layer(xc, w, consts), None

    out, _ = lax.scan(body, x, weights)
    return out
```

```python
import functools
import math

import jax
import jax.numpy as jnp
from jax import lax
from jax.experimental import pallas as pl
from jax.experimental.pallas import tpu as pltpu

f32 = jnp.float32
bf16 = jnp.bfloat16

N_HEADS = 16
QK_NOPE = 64
QK_ROPE = 32
QK_DIM = QK_NOPE + QK_ROPE
V_DIM = 64
Q_LORA = 384
KV_LORA = 256
ROPE_THETA = 10000.0
HY_W = 1024
HY_ORDER = 2
FILT_EMB = 33
FILT_HID = 64
DECAY_TARGET = 1e-2
FAST_DECAY = 0.3
SLOW_DECAY = 1.5
N_EXPERTS = 16
EC_FACTOR = 2
RMS_EPS = 1e-6

COL_Q = 0
COL_KV = COL_Q + Q_LORA
COL_KROPE = COL_KV + KV_LORA
COL_HYENA = COL_KROPE + QK_ROPE
COL_GATE = COL_HYENA + (HY_ORDER + 1) * HY_W

LANES = 128
SUBLANES = 8
DFT_MINOR = 128
GATHER_WIN = 144
DFT_STEP_PLANES = 2
VT_ROWS = 80
VMEM_LIMIT = 56 * 1024 * 1024


def _cp(sem, vmem=VMEM_LIMIT):
    return pltpu.CompilerParams(dimension_semantics=sem, vmem_limit_bytes=vmem)


def _tile(n, pref):
    t = min(n, pref)
    assert n % t == 0, (n, pref)
    return t


def _dot(a, b):
    return jnp.dot(a, b, preferred_element_type=f32)


def _dot_hi(a, b):
    return jnp.dot(a, b, preferred_element_type=f32, precision=lax.Precision.HIGHEST)


def _in_proj_kernel(x_ref, nw_ref, wa_ref, wu_ref, wg_ref, a_ref, u_ref, g_ref):
    x = x_ref[...]
    ms = jnp.mean(x * x, axis=-1, keepdims=True)
    xn = (x * lax.rsqrt(ms + RMS_EPS) * nw_ref[...]).astype(bf16)
    a_ref[...] = _dot(xn, wa_ref[...])
    u_ref[...] = _dot(xn, wu_ref[...]).astype(bf16)
    g_ref[...] = jax.nn.sigmoid(_dot(xn, wg_ref[...])).astype(bf16)


def _in_proj(x2, nw, wa, wu, wg):
    T, D = x2.shape
    tm = _tile(T, 256)
    na, nu, ng = wa.shape[1], wu.shape[1], wg.shape[1]
    full = lambda i: (0, 0)
    row = lambda i: (i, 0)
    return pl.pallas_call(
        _in_proj_kernel,
        grid=(T // tm,),
        in_specs=[pl.BlockSpec((tm, D), row), pl.BlockSpec((1, D), full),
                  pl.BlockSpec((D, na), full), pl.BlockSpec((D, nu), full), pl.BlockSpec((D, ng), full)],
        out_specs=[pl.BlockSpec((tm, na), row), pl.BlockSpec((tm, nu), row), pl.BlockSpec((tm, ng), row)],
        out_shape=[jax.ShapeDtypeStruct((T, na), f32), jax.ShapeDtypeStruct((T, nu), bf16),
                   jax.ShapeDtypeStruct((T, ng), bf16)],
        compiler_params=_cp(("parallel",)),
        name="in_proj",
    )(x2, nw, wa, wu, wg)


def _head_norm_rope(t, w, ct, s1, s2):
    ss = jnp.sum(t * t, axis=-1, keepdims=True) * (1.0 / QK_DIM)
    t = t * lax.rsqrt(ss + RMS_EPS) * w
    return t * ct + pltpu.roll(t, LANES - QK_ROPE // 2, 1) * s1 + pltpu.roll(t, QK_ROPE // 2, 1) * s2


def _mla_prep_kernel(a_ref, qaw_ref, kvaw_ref, wq_ref, wk_ref, wv_ref, qnw_ref, knw_ref,
                     ct_ref, s1_ref, s2_ref, q_ref, k_ref, v_ref, *, qscale):
    a = a_ref[...]
    ql = a[:, :Q_LORA]
    kvl = a[:, Q_LORA:Q_LORA + KV_LORA]
    kr = a[:, Q_LORA + KV_LORA:]
    qn = (ql * lax.rsqrt(jnp.mean(ql * ql, axis=-1, keepdims=True) + RMS_EPS) * qaw_ref[...]).astype(bf16)
    kvn = (kvl * lax.rsqrt(jnp.mean(kvl * kvl, axis=-1, keepdims=True) + RMS_EPS) * kvaw_ref[...]).astype(bf16)
    q = _dot(qn, wq_ref[...])
    k = _dot(kvn, wk_ref[...])
    v = _dot(kvn, wv_ref[...])
    ct, s1, s2 = ct_ref[...], s1_ref[...], s2_ref[...]
    lane = lax.broadcasted_iota(jnp.int32, (1, LANES), 1)
    ones_col = (lane == V_DIM).astype(f32)
    for h in range(N_HEADS):
        sl = slice(h * LANES, (h + 1) * LANES)
        qh = _head_norm_rope(q[:, sl], qnw_ref[...], ct, s1, s2) * qscale
        kh = _head_norm_rope(k[:, sl] + kr, knw_ref[...], ct, s1, s2)
        q_ref[h] = qh.astype(bf16)
        k_ref[h] = kh.astype(bf16)
        v_ref[h] = (v[:, sl] + ones_col).T[:VT_ROWS].astype(bf16)


def _mla_prep(a3, qaw, kvaw, wq, wk, wv, qnw, knw, ct, s1, s2):
    B, S, NA = a3.shape
    tm = _tile(S, 512)
    HP = N_HEADS * LANES
    qscale = (QK_DIM ** -0.5) * math.log2(math.e)
    full2 = lambda b, i: (0, 0)
    pos = lambda b, i: (i, 0)
    out_spec = pl.BlockSpec((None, N_HEADS, tm, LANES), lambda b, i: (b, 0, i, 0))
    out_sds = jax.ShapeDtypeStruct((B, N_HEADS, S, LANES), bf16)
    return pl.pallas_call(
        functools.partial(_mla_prep_kernel, qscale=qscale),
        grid=(B, S // tm),
        in_specs=[pl.BlockSpec((None, tm, NA), lambda b, i: (b, i, 0)),
                  pl.BlockSpec((1, Q_LORA), full2), pl.BlockSpec((1, KV_LORA), full2),
                  pl.BlockSpec((Q_LORA, HP), full2), pl.BlockSpec((KV_LORA, HP), full2),
                  pl.BlockSpec((KV_LORA, HP), full2),
                  pl.BlockSpec((1, LANES), full2), pl.BlockSpec((1, LANES), full2),
                  pl.BlockSpec((tm, LANES), pos), pl.BlockSpec((tm, LANES), pos), pl.BlockSpec((tm, LANES), pos)],
        out_specs=[out_spec, out_spec,
                   pl.BlockSpec((None, N_HEADS, None, VT_ROWS, tm), lambda b, i: (b, 0, i, 0, 0))],
        out_shape=[out_sds, out_sds, jax.ShapeDtypeStruct((B, N_HEADS, S // tm, VT_ROWS, tm), bf16)],
        compiler_params=_cp(("parallel", "parallel")),
        name="mla_prep",
    )(a3, qaw, kvaw, wq, wk, wv, qnw, knw, ct, s1, s2)


def _flash_kernel(q_ref, k_ref, vt_ref, o_ref, m_sc, acc_sc, s0_sc, s1_sc, *, unroll, online_max):
    nchunk, _, ck = vt_ref.shape
    q = q_ref[...]
    m_sc[...] = jnp.full(m_sc.shape, -jnp.inf, f32)
    acc_sc[...] = jnp.zeros(acc_sc.shape, f32)
    sbuf = (s0_sc, s1_sc)

    def scores(c, buf):
        start = pl.multiple_of(c * ck, ck)
        buf[...] = lax.dot_general(k_ref[pl.ds(start, ck), :], q, (((1,), (1,)), ((), ())),
                                   preferred_element_type=f32)

    def softmax_pv(c, buf):
        st = buf[...]
        if online_max:
            m = m_sc[...]
            m_new = jnp.maximum(m, jnp.max(st, axis=0, keepdims=True))
            pt = jnp.exp2(st - m_new).astype(bf16)
            acc_sc[0:VT_ROWS, :] = jnp.exp2(m - m_new) * acc_sc[0:VT_ROWS, :] + _dot(vt_ref[c], pt)
            m_sc[...] = m_new
        else:
            acc_sc[0:VT_ROWS, :] += _dot(vt_ref[c], jnp.exp2(st).astype(bf16))

    scores(0, s0_sc)

    def body(i, carry):
        for u in range(unroll):
            c = i * unroll + u
            scores(jnp.minimum(c + 1, nchunk - 1), sbuf[(u + 1) % 2])
            softmax_pv(c, sbuf[u % 2])
        return carry

    lax.fori_loop(0, nchunk // unroll, body, 0)
    acc = acc_sc[...].T
    o_ref[...] = (acc[:, :V_DIM] / acc[:, V_DIM:V_DIM + 1]).astype(o_ref.dtype)


SCORE_BOUND_LIMIT = 90.0


def _flash(q, k, vt, score_bound):
    B, H, S, _ = q.shape
    nchunk, ck = vt.shape[2], vt.shape[4]
    tq = _tile(S, 512)
    unroll = 32 if nchunk % 32 == 0 else (4 if nchunk % 4 == 0 else 2)
    assert nchunk % unroll == 0

    def call(online_max):
        return pl.pallas_call(
            functools.partial(_flash_kernel, unroll=unroll, online_max=online_max),
            grid=(B, H, S // tq),
            in_specs=[pl.BlockSpec((None, None, tq, LANES), lambda b, h, i: (b, h, i, 0)),
                      pl.BlockSpec((None, None, S, LANES), lambda b, h, i: (b, h, 0, 0)),
                      pl.BlockSpec((None, None, nchunk, VT_ROWS, ck), lambda b, h, i: (b, h, 0, 0, 0))],
            out_specs=pl.BlockSpec((None, None, tq, V_DIM), lambda b, h, i: (b, h, i, 0)),
            out_shape=jax.ShapeDtypeStruct((B, H, S, V_DIM), bf16),
            scratch_shapes=[pltpu.VMEM((1, tq), f32), pltpu.VMEM((LANES, tq), f32),
                            pltpu.VMEM((ck, tq), f32), pltpu.VMEM((ck, tq), f32)],
            compiler_params=_cp(("parallel", "parallel", "parallel")),
            name="flash_attn_online" if online_max else "flash_attn_bounded")

    return lax.cond(score_bound <= SCORE_BOUND_LIMIT, call(False), call(True), q, k, vt)


def _short_conv_kernel(u_ref, prev_ref, next_ref, w_ref, b_ref, o_ref):
    jb = pl.program_id(1)
    last = pl.num_programs(1) - 1
    nplanes, nh, _ = u_ref.shape
    rows = lax.broadcasted_iota(jnp.int32, (nh, 1), 0)
    prev = prev_ref[...].astype(f32)
    nxt = next_ref[...].astype(f32)
    prev_wrap = jnp.where(rows == 0, 0.0, pltpu.roll(prev, 1, 0))
    next_wrap = jnp.where(rows == nh - 1, 0.0, pltpu.roll(nxt, nh - 1, 0))
    halo_up = jnp.where(jb == 0, prev_wrap, prev)
    halo_dn = jnp.where(jb == last, next_wrap, nxt)
    w = w_ref[...]
    bias = b_ref[...]
    for p in range(nplanes):
        up = halo_up if p == 0 else u_ref[p - 1].astype(f32)
        dn = halo_dn if p == nplanes - 1 else u_ref[p + 1].astype(f32)
        o_ref[p] = (bias + up * w[0:1, :] + u_ref[p].astype(f32) * w[1:2, :] + dn * w[2:3, :]).astype(o_ref.dtype)


def _short_conv(ut, w, b):
    B, P, nh, CT = ut.shape
    tc = _tile(CT, 1024)
    J = 8
    blk = pl.BlockSpec((None, J, nh, tc), lambda bi, j, c: (bi, j, 0, c))
    halo = lambda off: pl.BlockSpec((None, None, nh, tc), lambda bi, j, c: (bi, (j * J + off + P) % P, 0, c))
    return pl.pallas_call(
        _short_conv_kernel,
        grid=(B, P // J, CT // tc),
        in_specs=[blk, halo(-1), halo(J),
                  pl.BlockSpec((3, tc), lambda bi, j, c: (0, c)),
                  pl.BlockSpec((1, tc), lambda bi, j, c: (0, c))],
        out_specs=blk,
        out_shape=jax.ShapeDtypeStruct((B, P, nh, CT), bf16),
        compiler_params=_cp(("parallel", "parallel", "parallel")),
        name="short_conv",
    )(ut, ut, ut, w, b)


def _filter_kernel(feat_ref, tau_ref, w1_ref, b1_ref, f1_ref, w2_ref, b2_ref, f2_ref, w3_ref, dl_ref,
                   h_ref, l1_ref):
    i = pl.program_id(0)
    h = jnp.sin(f1_ref[...] * (_dot_hi(feat_ref[...], w1_ref[...]) + b1_ref[...]))
    h = jnp.sin(f2_ref[...] * (_dot_hi(h, w2_ref[...]) + b2_ref[...]))
    h = _dot_hi(h, w3_ref[...])
    h = h * jnp.exp(-tau_ref[...] * dl_ref[...])
    h_ref[...] = h

    @pl.when(i == 0)
    def _():
        l1_ref[...] = jnp.zeros(l1_ref.shape, f32)

    l1_ref[...] += jnp.sum(jnp.abs(h), axis=0, keepdims=True)


def _filters(feats, tau, w1, b1, f1, w2, b2, f2, w3, dl):
    S = feats.shape[0]
    CO = w3.shape[1]
    tm = _tile(S, 512)
    full = lambda i: (0, 0)
    row = lambda i: (i, 0)
    return pl.pallas_call(
        _filter_kernel,
        grid=(S // tm,),
        in_specs=[pl.BlockSpec((tm, LANES), row), pl.BlockSpec((tm, 1), row),
                  pl.BlockSpec((LANES, LANES), full), pl.BlockSpec((1, LANES), full), pl.BlockSpec((1, LANES), full),
                  pl.BlockSpec((LANES, LANES), full), pl.BlockSpec((1, LANES), full), pl.BlockSpec((1, LANES), full),
                  pl.BlockSpec((LANES, CO), full), pl.BlockSpec((1, CO), full)],
        out_specs=[pl.BlockSpec((tm, CO), row), pl.BlockSpec((1, CO), full)],
        out_shape=[jax.ShapeDtypeStruct((S, CO), f32), jax.ShapeDtypeStruct((1, CO), f32)],
        compiler_params=_cp(("arbitrary",)),
        name="hyena_filters",
    )(feats, tau, w1, b1, f1, w2, b2, f2, w3, dl)


def _pack_pair(re, im):
    hi = pltpu.bitcast(re.astype(bf16).astype(f32), jnp.uint32) & jnp.uint32(0xFFFF0000)
    lo = pltpu.bitcast(im.astype(bf16).astype(f32), jnp.uint32) >> 16
    return hi | lo


def _unpack_pair(w):
    re = pltpu.bitcast(w & jnp.uint32(0xFFFF0000), f32)
    im = pltpu.bitcast(w << 16, f32)
    return re.astype(bf16), im.astype(bf16)


def _dft_a_kernel(x_ref, m_ref, tc_ref, ts_ref, a_ref, *, packed):
    n1f = m_ref.shape[0] // 2
    m = m_ref[...]
    for j in range(tc_ref.shape[0]):
        if packed:
            pr = _dot(m, x_ref[0, j].astype(bf16))
            pi = _dot(m, x_ref[1, j].astype(bf16))
            ar = pr[:n1f] + pi[n1f:]
            ai = pi[:n1f] - pr[n1f:]
        else:
            p = _dot(m, x_ref[j].astype(bf16))
            ar = p[:n1f]
            ai = -p[n1f:]
        tc, ts = tc_ref[j], ts_ref[j]
        a_ref[:, j, :] = _pack_pair(ar * tc + ai * ts, ai * tc - ar * ts)


def _dft_a(x4, col_blk, C, m1, twc, tws, packed):
    n1f = m1.shape[0] // 2
    nh = m1.shape[1]
    J = SUBLANES
    cb = _tile(C, 1024)
    if packed:
        x_spec = pl.BlockSpec((2, J, nh, cb), lambda j, c: (0, j, 0, col_blk + c))
    else:
        x_spec = pl.BlockSpec((J, nh, cb), lambda j, c: (j, 0, col_blk + c))
    return pl.pallas_call(
        functools.partial(_dft_a_kernel, packed=packed),
        grid=(DFT_MINOR // J, C // cb),
        in_specs=[x_spec,
                  pl.BlockSpec((2 * n1f, nh), lambda j, c: (0, 0)),
                  pl.BlockSpec((J, n1f, 1), lambda j, c: (j, 0, 0)),
                  pl.BlockSpec((J, n1f, 1), lambda j, c: (j, 0, 0))],
        out_specs=pl.BlockSpec((n1f, J, cb), lambda j, c: (0, j, c)),
        out_shape=jax.ShapeDtypeStruct((n1f, DFT_MINOR, C), jnp.uint32),
        compiler_params=_cp(("parallel", "parallel")),
        name="dft_stage_a",
    )(x4, m1, twc, tws)


def _dft_bf_kernel(a_ref, m2_ref, inv_ref, h_ref):
    C = a_ref.shape[-1]
    for j in range(a_ref.shape[0]):
        a = jnp.concatenate(_unpack_pair(a_ref[j]), axis=0)
        z = _dot(m2_ref[...], a) * inv_ref[...]
        h_ref[j] = z.reshape(2, DFT_MINOR, C).astype(h_ref.dtype)


def _dft_b_filter(a3, m2f, inv_l1):
    n1f, _, C = a3.shape
    J = DFT_STEP_PLANES
    return pl.pallas_call(
        _dft_bf_kernel,
        grid=(n1f // J,),
        in_specs=[pl.BlockSpec((J, DFT_MINOR, C), lambda k: (k, 0, 0)),
                  pl.BlockSpec((2 * DFT_MINOR, 2 * DFT_MINOR), lambda k: (0, 0)),
                  pl.BlockSpec((1, C), lambda k: (0, 0))],
        out_specs=pl.BlockSpec((J, 2, DFT_MINOR, C), lambda k: (k, 0, 0, 0)),
        out_shape=jax.ShapeDtypeStruct((n1f, 2, DFT_MINOR, C), bf16),
        compiler_params=_cp(("parallel",)),
        name="dft_stage_b_filter",
    )(a3, m2f, inv_l1)


def _dft_b_kernel(a_ref, h_ref, m2f_ref, m2i_ref, tc_ref, ts_ref, b_ref):
    for j in range(a_ref.shape[0]):
        a = jnp.concatenate(_unpack_pair(a_ref[j]), axis=0)
        z = _dot(m2f_ref[...], a)
        zr, zi = z[:DFT_MINOR], z[DFT_MINOR:]
        hr, hi = h_ref[j, 0].astype(f32), h_ref[j, 1].astype(f32)
        y = jnp.concatenate([zr * hr - zi * hi, zr * hi + zi * hr], axis=0).astype(bf16)
        w = _dot(m2i_ref[...], y)
        wr, wi = w[:DFT_MINOR], w[DFT_MINOR:]
        tc, ts = tc_ref[j], ts_ref[j]
        b_ref[:, j, :] = _pack_pair(wr * tc - wi * ts, wr * ts + wi * tc)


def _dft_b(a3, hspec, order, m2f, m2i, twc_t, tws_t):
    n1f, _, C = a3.shape
    J = SUBLANES
    return pl.pallas_call(
        _dft_b_kernel,
        grid=(n1f // J,),
        in_specs=[pl.BlockSpec((J, DFT_MINOR, C), lambda k: (k, 0, 0)),
                  pl.BlockSpec((J, 2, DFT_MINOR, C), lambda k: (k, 0, 0, order)),
                  pl.BlockSpec((2 * DFT_MINOR, 2 * DFT_MINOR), lambda k: (0, 0)),
                  pl.BlockSpec((2 * DFT_MINOR, 2 * DFT_MINOR), lambda k: (0, 0)),
                  pl.BlockSpec((J, DFT_MINOR, 1), lambda k: (k, 0, 0)),
                  pl.BlockSpec((J, DFT_MINOR, 1), lambda k: (k, 0, 0))],
        out_specs=pl.BlockSpec((DFT_MINOR, J, C), lambda k: (0, k, 0)),
        out_shape=jax.ShapeDtypeStruct((DFT_MINOR, n1f, C), jnp.uint32),
        compiler_params=_cp(("parallel",)),
        name="dft_stage_b",
    )(a3, hspec, m2f, m2i, twc_t, tws_t)


def _dft_c_kernel(b_ref, g_ref, z_ref, xg_ref, skip_ref, o_ref):
    n1f = b_ref.shape[1]
    nh = n1f // 2
    skip = skip_ref[...]
    for j in range(b_ref.shape[0]):
        bb = jnp.concatenate(_unpack_pair(b_ref[j]), axis=0)
        y = _dot(g_ref[...], bb)
        for bi in range(2):
            z = z_ref[bi, j].astype(f32)
            o_ref[bi, j] = (xg_ref[bi, j].astype(f32) * (y[bi * nh:(bi + 1) * nh] + z * skip)).astype(o_ref.dtype)


def _dft_c(b3, gm, z4, z_blk, xg4, xg_blk, skip):
    _, n1f, C = b3.shape
    nh = n1f // 2
    J = DFT_STEP_PLANES
    return pl.pallas_call(
        _dft_c_kernel,
        grid=(DFT_MINOR // J,),
        in_specs=[pl.BlockSpec((J, n1f, C), lambda j: (j, 0, 0)),
                  pl.BlockSpec((n1f, 2 * n1f), lambda j: (0, 0)),
                  pl.BlockSpec((2, J, nh, C), lambda j: (0, j, 0, z_blk)),
                  pl.BlockSpec((2, J, nh, C), lambda j: (0, j, 0, xg_blk)),
                  pl.BlockSpec((1, C), lambda j: (0, 0))],
        out_specs=pl.BlockSpec((2, J, nh, C), lambda j: (0, j, 0, 0)),
        out_shape=jax.ShapeDtypeStruct((2, DFT_MINOR, nh, C), bf16),
        compiler_params=_cp(("parallel",)),
        name="dft_stage_c",
    )(b3, gm, z4, xg4, skip)


def _dft_tables(S):
    N = 2 * S
    n1f = N // DFT_MINOR
    nh = n1f // 2
    two_pi = 2.0 * math.pi

    def cs(num, den):
        ang = (num % den).astype(f32) * (two_pi / den)
        return jnp.cos(ang), jnp.sin(ang)

    k1 = jnp.arange(n1f, dtype=jnp.int32)
    n1 = jnp.arange(nh, dtype=jnp.int32)
    c1, s1 = cs(k1[:, None] * n1[None, :], n1f)
    m1 = jnp.concatenate([c1, s1], axis=0).astype(bf16)
    n2 = jnp.arange(DFT_MINOR, dtype=jnp.int32)
    twc, tws = cs(n2[:, None] * k1[None, :], N)
    c2, s2 = cs(n2[:, None] * n2[None, :], DFT_MINOR)
    m2f = jnp.concatenate([jnp.concatenate([c2, s2], 1), jnp.concatenate([-s2, c2], 1)], 0).astype(bf16)
    m2i = jnp.concatenate([jnp.concatenate([c2, -s2], 1), jnp.concatenate([s2, c2], 1)], 0).astype(bf16)
    rows = jnp.arange(nh, dtype=jnp.int32) + n1f // 4
    gc, gs = cs(rows[:, None] * k1[None, :], n1f)
    gm = (jnp.concatenate([jnp.concatenate([gc, -gs], 1), jnp.concatenate([gs, gc], 1)], 0) / N).astype(bf16)
    return dict(m1=m1, twc=twc[:, :, None], tws=tws[:, :, None],
                twc_t=twc.T[:, :, None], tws_t=tws.T[:, :, None], m2f=m2f, m2i=m2i, gm=gm)


def _merge_kernel(x_ref, at_ref, hy_ref, g_ref, wa_ref, wh_ref, wo_ref, n2_ref, wr_ref,
                  xo_ref, hn_ref, aff_ref):
    D = x_ref.shape[1]
    g = g_ref[...].astype(f32)
    merged = g[:, :D] * _dot(at_ref[...], wa_ref[...]) + g[:, D:] * _dot(hy_ref[...], wh_ref[...])
    x = x_ref[...] + _dot(merged.astype(bf16), wo_ref[...])
    xo_ref[...] = x
    hn = x * lax.rsqrt(jnp.mean(x * x, axis=-1, keepdims=True) + RMS_EPS) * n2_ref[...]
    hn_ref[...] = hn.astype(bf16)
    logits = _dot_hi(hn, wr_ref[...])
    lane = lax.broadcasted_iota(jnp.int32, logits.shape, 1)
    logits = jnp.where(lane < N_EXPERTS, logits, -jnp.inf)
    e = jnp.exp(logits - jnp.max(logits, axis=-1, keepdims=True))
    aff_ref[...] = e / jnp.sum(e, axis=-1, keepdims=True)


def _merge(x2, attn, hy, g, wa, wh, wo, n2w, wr):
    T, D = x2.shape
    tm = _tile(T, 512)
    row = lambda i: (i, 0)
    full = lambda i: (0, 0)
    return pl.pallas_call(
        _merge_kernel,
        grid=(T // tm,),
        in_specs=[pl.BlockSpec((tm, D), row), pl.BlockSpec((tm, D), row), pl.BlockSpec((tm, D), row),
                  pl.BlockSpec((tm, 2 * D), row),
                  pl.BlockSpec((D, D), full), pl.BlockSpec((D, D), full), pl.BlockSpec((D, D), full),
                  pl.BlockSpec((1, D), full), pl.BlockSpec((D, LANES), full)],
        out_specs=[pl.BlockSpec((tm, D), row), pl.BlockSpec((tm, D), row), pl.BlockSpec((tm, LANES), row)],
        out_shape=[jax.ShapeDtypeStruct((T, D), f32), jax.ShapeDtypeStruct((T, D), bf16),
                   jax.ShapeDtypeStruct((T, LANES), f32)],
        compiler_params=_cp(("parallel",)),
        name="merge_out_router",
    )(x2, attn, hy, g, wa, wh, wo, n2w, wr)


def _thresh_kernel(a_ref, thr_ref, need_ref, *, cap):
    bits = pltpu.bitcast(a_ref[...], jnp.int32)
    R = bits.shape[0]

    def body(i, thr):
        cand = thr | jnp.left_shift(jnp.int32(1), 30 - i)
        cnt = jnp.sum((bits >= cand).astype(f32), axis=1, keepdims=True)
        return jnp.where(cnt >= cap, cand, thr)

    thr = lax.fori_loop(0, 31, body, jnp.zeros((R, 1), jnp.int32))
    n_gt = jnp.sum((bits > thr).astype(f32), axis=1, keepdims=True)
    thr_ref[...] = jnp.broadcast_to(pltpu.bitcast(thr, f32), thr_ref.shape)
    need_ref[...] = jnp.broadcast_to(cap - n_gt, need_ref.shape)


def _thresholds(aff_t, cap):
    R, S = aff_t.shape
    return pl.pallas_call(
        functools.partial(_thresh_kernel, cap=float(cap)),
        out_shape=[jax.ShapeDtypeStruct((R, LANES), f32), jax.ShapeDtypeStruct((R, LANES), f32)],
        compiler_params=_cp(None),
        name="route_threshold",
    )(aff_t)


def _route_kernel(aff_ref, thr_ref, need_ref, ltri_ref, slab_ref, sel_ref, pos_ref, base_ref, ceq_sc, csel_sc):
    k = pl.program_id(1)

    @pl.when(k == 0)
    def _():
        ceq_sc[...] = jnp.zeros(ceq_sc.shape, f32)
        csel_sc[...] = jnp.zeros(csel_sc.shape, f32)

    a = aff_ref[...]
    thr = thr_ref[...]
    gt = a > thr
    eq = a == thr
    eqf = eq.astype(f32)
    ltri = ltri_ref[...]
    eq_rank = ceq_sc[...] + _dot(ltri, eqf.astype(bf16))
    sel = jnp.logical_or(gt, jnp.logical_and(eq, eq_rank < need_ref[...]))
    self_ = sel.astype(f32)
    pos = csel_sc[...] + _dot(ltri, self_.astype(bf16))
    sel_ref[...] = self_
    pos_ref[...] = pos
    base_ref[...] = csel_sc[...] + _dot(slab_ref[...], self_.astype(bf16))
    ceq_sc[...] += jnp.sum(eqf, axis=0, keepdims=True)
    csel_sc[...] += jnp.sum(self_, axis=0, keepdims=True)


def _route(aff3, thr3, need3):
    B, S, _ = aff3.shape
    tk = _tile(S, 1024)
    ltri = (jnp.arange(tk)[:, None] > jnp.arange(tk)[None, :]).astype(bf16)
    tok = pl.BlockSpec((None, tk, LANES), lambda b, k: (b, k, 0))
    per_b = pl.BlockSpec((None, 1, LANES), lambda b, k: (b, 0, 0))
    return pl.pallas_call(
        _route_kernel,
        grid=(B, S // tk),
        in_specs=[tok, per_b, per_b, pl.BlockSpec((tk, tk), lambda b, k: (0, 0)),
                  pl.BlockSpec((tk // LANES, tk), lambda b, k: (0, 0))],
        out_specs=[tok, tok, pl.BlockSpec((None, tk // LANES, LANES), lambda b, k: (b, k, 0))],
        out_shape=[jax.ShapeDtypeStruct((B, S, LANES), f32), jax.ShapeDtypeStruct((B, S, LANES), f32),
                   jax.ShapeDtypeStruct((B, S // LANES, LANES), f32)],
        scratch_shapes=[pltpu.VMEM((1, LANES), f32), pltpu.VMEM((1, LANES), f32)],
        compiler_params=_cp(("parallel", "arbitrary")),
        name="route_positions",
    )(aff3, thr3, need3, ltri, ltri[::LANES])


def _gather_kernel(base_sm, hn_ref, pos_ref, sel_ref, gate_ref, xs_ref, gs_ref, acc_sc, gacc_sc, *, nsub, nsub_tot):
    b, e, k = pl.program_id(0), pl.program_id(1), pl.program_id(2)
    cap = xs_ref.shape[0]

    @pl.when(k == 0)
    def _():
        acc_sc[...] = jnp.zeros(acc_sc.shape, f32)
        gacc_sc[...] = jnp.zeros(gacc_sc.shape, f32)

    rows = lax.broadcasted_iota(jnp.int32, (GATHER_WIN, LANES), 0).astype(f32)
    for j in range(nsub):
        base = base_sm[(b * N_EXPERTS + e) * nsub_tot + k * nsub + j]
        start = pl.multiple_of((base // SUBLANES) * SUBLANES, SUBLANES)
        sl = slice(j * LANES, (j + 1) * LANES)
        rel = pos_ref[:, sl] - start.astype(f32)
        hit = jnp.logical_and(rows == rel, sel_ref[:, sl] > 0.0)
        oht = jnp.where(hit, 1.0, 0.0)
        acc_sc[pl.ds(start, GATHER_WIN), :] += _dot(oht.astype(bf16), hn_ref[sl, :])
        gacc_sc[pl.ds(start, GATHER_WIN), :] += jnp.sum(oht * gate_ref[:, sl], axis=1, keepdims=True)

    @pl.when(k == pl.num_programs(2) - 1)
    def _():
        xs_ref[...] = acc_sc[0:cap, :].astype(xs_ref.dtype)
        gs_ref[...] = gacc_sc[0:cap, :]


def _gather(base_flat, hn3, pos_t, sel_t, gate_t, cap):
    B, S, D = hn3.shape
    tk = _tile(S, 1024)
    nsub = tk // LANES
    row = pl.BlockSpec((None, None, 1, tk), lambda b, e, k, base: (b, e, 0, k))
    return pl.pallas_call(
        functools.partial(_gather_kernel, nsub=nsub, nsub_tot=S // LANES),
        grid_spec=pltpu.PrefetchScalarGridSpec(
            num_scalar_prefetch=1,
            grid=(B, N_EXPERTS, S // tk),
            in_specs=[pl.BlockSpec((None, tk, D), lambda b, e, k, base: (b, k, 0)), row, row, row],
            out_specs=[pl.BlockSpec((None, None, cap, D), lambda b, e, k, base: (b, e, 0, 0)),
                       pl.BlockSpec((None, None, cap, 1), lambda b, e, k, base: (b, e, 0, 0))],
            scratch_shapes=[pltpu.VMEM((cap + GATHER_WIN, D), f32), pltpu.VMEM((cap + GATHER_WIN, 1), f32)]),
        out_shape=[jax.ShapeDtypeStruct((B, N_EXPERTS, cap, D), bf16),
                   jax.ShapeDtypeStruct((B, N_EXPERTS, cap, 1), f32)],
        compiler_params=_cp(("parallel", "parallel", "arbitrary")),
        name="moe_gather",
    )(base_flat, hn3, pos_t, sel_t, gate_t)


def _ffn_kernel(xs_ref, gs_ref, wg_ref, wu_ref, wd_ref, o_ref):
    xs = xs_ref[...]
    a = _dot(xs, wg_ref[...])
    u = _dot(xs, wu_ref[...])
    hmid = (a * jax.nn.sigmoid(a) * u).astype(bf16)
    o_ref[...] = (_dot(hmid, wd_ref[...]) * gs_ref[...]).astype(o_ref.dtype)


def _ffn(xs, gs, wg, wu, wd):
    B, E, cap, D = xs.shape
    F = wg.shape[-1]
    tm = _tile(cap, 512)
    return pl.pallas_call(
        _ffn_kernel,
        grid=(E, B, cap // tm),
        in_specs=[pl.BlockSpec((None, None, tm, D), lambda e, b, i: (b, e, i, 0)),
                  pl.BlockSpec((None, None, tm, 1), lambda e, b, i: (b, e, i, 0)),
                  pl.BlockSpec((None, D, F), lambda e, b, i: (e, 0, 0)),
                  pl.BlockSpec((None, D, F), lambda e, b, i: (e, 0, 0)),
                  pl.BlockSpec((None, F, D), lambda e, b, i: (e, 0, 0))],
        out_specs=pl.BlockSpec((None, None, tm, D), lambda e, b, i: (b, e, i, 0)),
        out_shape=jax.ShapeDtypeStruct((B, E, cap, D), bf16),
        compiler_params=_cp(("parallel", "parallel", "parallel")),
        name="expert_ffn",
    )(xs, gs, wg, wu, wd)


def _scatter_kernel(base_sm, x_ref, pos_ref, sel_ref, o_hbm, out_ref, buf, sem, *, nsub_tot, cap):
    g = pl.program_id(0)

    def window(step, e):
        b, k = step // nsub_tot, step % nsub_tot
        base = base_sm[(b * N_EXPERTS + e) * nsub_tot + k]
        start = jnp.minimum((base // SUBLANES) * SUBLANES, cap - GATHER_WIN)
        start = pl.multiple_of(start, SUBLANES)
        slot = step % 2
        return start, pltpu.make_async_copy(o_hbm.at[b, e, pl.ds(start, GATHER_WIN), :],
                                            buf.at[slot, e], sem.at[slot, e])

    def fetch(step):
        for e in range(N_EXPERTS):
            window(step, e)[1].start()

    @pl.when(g == 0)
    def _():
        fetch(g)

    @pl.when(g + 1 < pl.num_programs(0))
    def _():
        fetch(g + 1)

    acc = x_ref[...]
    pos = pos_ref[...]
    sel = sel_ref[...]
    lanes = lax.broadcasted_iota(jnp.int32, (LANES, GATHER_WIN), 1).astype(f32)
    for e in range(N_EXPERTS):
        start, cp = window(g, e)
        cp.wait()
        rel = pos[:, e:e + 1] - start.astype(f32)
        hit = jnp.logical_and(lanes == rel, sel[:, e:e + 1] > 0.0)
        acc = acc + _dot(jnp.where(hit, 1.0, 0.0).astype(bf16), buf[g % 2, e])
    out_ref[...] = acc


def _scatter(base_flat, x3, pos3, sel3, o4):
    B, S, D = x3.shape
    cap = o4.shape[2]
    nsub = S // LANES
    assert cap >= GATHER_WIN
    tok = lambda g, base: (g // nsub, g % nsub, 0)
    return pl.pallas_call(
        functools.partial(_scatter_kernel, nsub_tot=nsub, cap=cap),
        grid_spec=pltpu.PrefetchScalarGridSpec(
            num_scalar_prefetch=1,
            grid=(B * nsub,),
            in_specs=[pl.BlockSpec((None, LANES, D), tok), pl.BlockSpec((None, LANES, LANES), tok),
                      pl.BlockSpec((None, LANES, LANES), tok), pl.BlockSpec(memory_space=pl.ANY)],
            out_specs=pl.BlockSpec((None, LANES, D), tok),
            scratch_shapes=[pltpu.VMEM((2, N_EXPERTS, GATHER_WIN, D), bf16),
                            pltpu.SemaphoreType.DMA((2, N_EXPERTS))]),
        out_shape=jax.ShapeDtypeStruct((B, S, D), f32),
        compiler_params=_cp(("arbitrary",)),
        name="moe_scatter",
    )(base_flat, x3, pos3, sel3, o4)


def _pad_last(a, n):
    return jnp.pad(a, [(0, 0)] * (a.ndim - 1) + [(0, n - a.shape[-1])])


def _rope_tables(S):
    pos = jnp.arange(S, dtype=f32)
    inv_freq = ROPE_THETA ** (-jnp.arange(0, QK_ROPE, 2, dtype=f32) / QK_ROPE)
    ang = pos[:, None] * inv_freq[None, :]
    cos, sin = jnp.cos(ang), jnp.sin(ang)
    half = QK_ROPE // 2
    z_lo = jnp.zeros((S, QK_NOPE), f32)
    z_hi = jnp.zeros((S, LANES - QK_DIM), f32)
    z_half = jnp.zeros((S, half), f32)
    ct = jnp.concatenate([jnp.ones((S, QK_NOPE), f32), cos, cos, z_hi], axis=1)
    s1 = jnp.concatenate([z_lo, -sin, z_half, z_hi], axis=1)
    s2 = jnp.concatenate([z_lo, z_half, sin, z_hi], axis=1)
    return ct, s1, s2


def _hyena_positions(L):
    pos = jnp.arange(L, dtype=f32)
    t = pos / L
    bands = (FILT_EMB - 1) // 2
    f = jnp.linspace(1e-4, bands - 1, bands, dtype=f32)
    w = 2.0 * math.pi * t[:, None] * f[None, :]
    feats = jnp.concatenate([t[:, None], jnp.cos(w), jnp.sin(w)], axis=-1)
    tau = jnp.abs(pos - (L // 2)) / max(L // 2, 1)
    max_decay = math.log(DECAY_TARGET) / FAST_DECAY
    min_decay = math.log(DECAY_TARGET) / SLOW_DECAY
    deltas = jnp.abs(jnp.linspace(min_decay, max_decay, HY_W, dtype=f32))
    planes = lambda t: t.reshape(L // DFT_MINOR, DFT_MINOR, -1).transpose(1, 0, 2).reshape(L, -1)
    return planes(_pad_last(feats, LANES)), planes(tau[:, None]), jnp.tile(deltas, HY_ORDER)[None, :]


def _prep_weights(p):
    L = p["w_in"].shape[0]
    D = p["w_in"].shape[1]
    w_in = p["w_in"]
    kr = jnp.zeros((L, D, LANES), f32).at[:, :, QK_NOPE:QK_DIM].set(w_in[:, :, COL_KROPE:COL_HYENA])
    wq = _pad_last(p["w_q_b"].reshape(L, Q_LORA, N_HEADS, QK_DIM), LANES).reshape(L, Q_LORA, N_HEADS * LANES)
    wkv = p["w_kv_b"].reshape(L, KV_LORA, N_HEADS, QK_NOPE + V_DIM)
    wk = _pad_last(wkv[..., :QK_NOPE], LANES).reshape(L, KV_LORA, N_HEADS * LANES)
    wv = _pad_last(wkv[..., QK_NOPE:], LANES).reshape(L, KV_LORA, N_HEADS * LANES)
    padh = LANES - FILT_HID
    return dict(
        norm1_w=p["norm1_w"][:, None, :],
        wa=jnp.concatenate([w_in[:, :, COL_Q:COL_KROPE], kr], axis=-1).astype(bf16),
        wu=w_in[:, :, COL_HYENA:COL_GATE].astype(bf16),
        wg=w_in[:, :, COL_GATE:].astype(bf16),
        qaw=p["q_a_norm_w"][:, None, :], kvaw=p["kv_a_norm_w"][:, None, :],
        wq=wq.astype(bf16), wk=wk.astype(bf16), wv=wv.astype(bf16),
        qnw=_pad_last(p["q_norm_w"], LANES)[:, None, :], knw=_pad_last(p["k_norm_w"], LANES)[:, None, :],
        scw=p["short_conv_w"], scb=p["short_conv_b"][:, None, :],
        fw1=jnp.pad(p["filt_w1"], ((0, 0), (0, LANES - FILT_EMB), (0, padh))),
        fb1=_pad_last(p["filt_b1"], LANES)[:, None, :], ff1=_pad_last(p["filt_freq1"], LANES)[:, None, :],
        fw2=jnp.pad(p["filt_w2"], ((0, 0), (0, padh), (0, padh))),
        fb2=_pad_last(p["filt_b2"], LANES)[:, None, :], ff2=_pad_last(p["filt_freq2"], LANES)[:, None, :],
        fw3=jnp.pad(p["filt_w3"], ((0, 0), (0, padh), (0, 0))),
        skip=p["filt_skip"],
        w_attn=p["w_attn_branch"].astype(bf16), w_hy=p["w_hyena_branch"].astype(bf16),
        w_out=p["w_out"].astype(bf16),
        norm2_w=p["norm2_w"][:, None, :], wr=_pad_last(p["router_w"], LANES),
        ewg=p["expert_w_gate"].astype(bf16), ewu=p["expert_w_up"].astype(bf16),
        ewd=p["expert_w_down"].astype(bf16),
    )


def _layer(x, w, consts):
    B, S, D = x.shape
    T = B * S
    E = N_EXPERTS
    cap = max(1, EC_FACTOR * S // E)
    nh = S // DFT_MINOR

    a, u, g = _in_proj(x.reshape(T, D), w["norm1_w"], w["wa"], w["wu"], w["wg"])

    q, k, v = _mla_prep(a.reshape(B, S, -1), w["qaw"], w["kvaw"], w["wq"], w["wk"], w["wv"],
                        w["qnw"], w["knw"], consts["ct"], consts["s1"], consts["s2"])
    score_bound = (QK_DIM * (QK_DIM ** -0.5) * math.log2(math.e) * 1.05
                   * jnp.max(jnp.abs(w["qnw"])) * jnp.max(jnp.abs(w["knw"])))
    attn = _flash(q, k, v, score_bound)
    attn = attn.transpose(0, 2, 1, 3).reshape(T, N_HEADS * V_DIM)

    ut = u.reshape(B, nh, DFT_MINOR, -1).transpose(0, 2, 1, 3)
    uct = _short_conv(ut, w["scw"], w["scb"])
    h_un, l1 = _filters(consts["feats"], consts["tau"], w["fw1"], w["fb1"], w["ff1"], w["fw2"], w["fb2"],
                        w["ff2"], w["fw3"], consts["deltas"])
    tb = consts["dft"]
    ha = _dft_a(h_un.reshape(DFT_MINOR, nh, HY_ORDER * HY_W), 0, HY_ORDER * HY_W,
                tb["m1"], tb["twc"], tb["tws"], packed=False)
    hspec = _dft_b_filter(ha, tb["m2f"], 1.0 / l1)
    z4, z_blk = uct, 0
    for n in range(HY_ORDER):
        za = _dft_a(z4, z_blk, HY_W, tb["m1"], tb["twc"], tb["tws"], packed=True)
        zb = _dft_b(za, hspec, n, tb["m2f"], tb["m2i"], tb["twc_t"], tb["tws_t"])
        z4 = _dft_c(zb, tb["gm"], z4, z_blk, uct, n + 1, w["skip"][n:n + 1])
        z_blk = 0
    hy = z4.transpose(0, 2, 1, 3).reshape(T, HY_W)

    x2, hn, aff = _merge(x.reshape(T, D), attn, hy, g, w["w_attn"], w["w_hy"], w["w_out"], w["norm2_w"], w["wr"])

    aff3 = aff.reshape(B, S, LANES)
    aff_t = aff3[:, :, :E].transpose(0, 2, 1)
    thr, need = _thresholds(aff_t.reshape(B * E, S), cap)
    pad_inf = jnp.full((B, LANES - E), jnp.inf, f32)
    thr3 = jnp.concatenate([thr[:, 0].reshape(B, E), pad_inf], axis=1)[:, None, :]
    need3 = _pad_last(need[:, 0].reshape(B, E), LANES)[:, None, :]
    sel, pos, base = _route(aff3, thr3, need3)
    base_flat = base[:, :, :E].transpose(0, 2, 1).reshape(-1).astype(jnp.int32)
    to_rows = lambda t: t[:, :, :E].transpose(0, 2, 1)[:, :, None, :]
    xs, gs = _gather(base_flat, hn.reshape(B, S, D), to_rows(pos), to_rows(sel), aff_t[:, :, None, :], cap)
    o = _ffn(xs, gs, w["ewg"], w["ewu"], w["ewd"])
    return _scatter(base_flat, x2.reshape(B, S, D), pos, sel, o)


def kernel(x, norm1_w, w_in, q_a_norm_w, w_q_b, kv_a_norm_w, w_kv_b, q_norm_w, k_norm_w, short_conv_w,
           short_conv_b, filt_w1, filt_b1, filt_freq1, filt_w2, filt_b2, filt_freq2, filt_w3, filt_skip,
           w_attn_branch, w_hyena_branch, w_out, norm2_w, router_w, expert_w_gate, expert_w_up, expert_w_down):
    B, S, D = x.shape
    assert S % (2 * DFT_MINOR) == 0 and w_hyena_branch.shape[1] == HY_W
    params = dict(norm1_w=norm1_w, w_in=w_in, q_a_norm_w=q_a_norm_w, w_q_b=w_q_b, kv_a_norm_w=kv_a_norm_w,
                  w_kv_b=w_kv_b, q_norm_w=q_norm_w, k_norm_w=k_norm_w, short_conv_w=short_conv_w,
                  short_conv_b=short_conv_b, filt_w1=filt_w1, filt_b1=filt_b1, filt_freq1=filt_freq1,
                  filt_w2=filt_w2, filt_b2=filt_b2, filt_freq2=filt_freq2, filt_w3=filt_w3, filt_skip=filt_skip,
                  w_attn_branch=w_attn_branch, w_hyena_branch=w_hyena_branch, w_out=w_out, norm2_w=norm2_w,
                  router_w=router_w, expert_w_gate=expert_w_gate, expert_w_up=expert_w_up,
                  expert_w_down=expert_w_down)
    weights = _prep_weights(params)
    ct, s1, s2 = _rope_tables(S)
    feats, tau, deltas = _hyena_positions(S)
    consts = dict(ct=ct, s1=s1, s2=s2, feats=feats, tau=tau, deltas=deltas, dft=_dft_tables(S))

    def body(xc, w):
        return _layer(xc, w, consts), None

    out, _ = lax.scan(body, x, weights)
    return out
```

```python
import functools
import math

import jax
import jax.numpy as jnp
from jax import lax
from jax.experimental import pallas as pl
from jax.experimental.pallas import tpu as pltpu

f32 = jnp.float32
bf16 = jnp.bfloat16

N_HEADS = 16
QK_NOPE = 64
QK_ROPE = 32
QK_DIM = QK_NOPE + QK_ROPE
V_DIM = 64
Q_LORA = 384
KV_LORA = 256
ROPE_THETA = 10000.0
HY_W = 1024
HY_ORDER = 2
FILT_EMB = 33
FILT_HID = 64
DECAY_TARGET = 1e-2
FAST_DECAY = 0.3
SLOW_DECAY = 1.5
N_EXPERTS = 16
EC_FACTOR = 2
RMS_EPS = 1e-6

COL_Q = 0
COL_KV = COL_Q + Q_LORA
COL_KROPE = COL_KV + KV_LORA
COL_HYENA = COL_KROPE + QK_ROPE
COL_GATE = COL_HYENA + (HY_ORDER + 1) * HY_W

LANES = 128
SUBLANES = 8
DFT_MINOR = 128
GATHER_WIN = 144
DFT_STEP_PLANES = 4
VT_ROWS = 80
VMEM_LIMIT = 56 * 1024 * 1024


def _cp(sem, vmem=VMEM_LIMIT):
    return pltpu.CompilerParams(dimension_semantics=sem, vmem_limit_bytes=vmem)


def _tile(n, pref):
    t = min(n, pref)
    assert n % t == 0, (n, pref)
    return t


def _dot(a, b):
    return jnp.dot(a, b, preferred_element_type=f32)


def _dot_hi(a, b):
    return jnp.dot(a, b, preferred_element_type=f32, precision=lax.Precision.HIGHEST)


def _in_proj_kernel(x_ref, nw_ref, wa_ref, wu_ref, wg_ref, a_ref, u_ref, g_ref):
    x = x_ref[...]
    ms = jnp.mean(x * x, axis=-1, keepdims=True)
    xn = (x * lax.rsqrt(ms + RMS_EPS) * nw_ref[...]).astype(bf16)
    a_ref[...] = _dot(xn, wa_ref[...])
    u_ref[...] = _dot(xn, wu_ref[...]).astype(bf16)
    g_ref[...] = jax.nn.sigmoid(_dot(xn, wg_ref[...])).astype(bf16)


def _in_proj(x2, nw, wa, wu, wg):
    T, D = x2.shape
    tm = _tile(T, 256)
    na, nu, ng = wa.shape[1], wu.shape[1], wg.shape[1]
    full = lambda i: (0, 0)
    row = lambda i: (i, 0)
    return pl.pallas_call(
        _in_proj_kernel,
        grid=(T // tm,),
        in_specs=[pl.BlockSpec((tm, D), row), pl.BlockSpec((1, D), full),
                  pl.BlockSpec((D, na), full), pl.BlockSpec((D, nu), full), pl.BlockSpec((D, ng), full)],
        out_specs=[pl.BlockSpec((tm, na), row), pl.BlockSpec((tm, nu), row), pl.BlockSpec((tm, ng), row)],
        out_shape=[jax.ShapeDtypeStruct((T, na), f32), jax.ShapeDtypeStruct((T, nu), bf16),
                   jax.ShapeDtypeStruct((T, ng), bf16)],
        compiler_params=_cp(("parallel",)),
        name="in_proj",
    )(x2, nw, wa, wu, wg)


def _head_norm_rope(t, w, ct, s1, s2):
    ss = jnp.sum(t * t, axis=-1, keepdims=True) * (1.0 / QK_DIM)
    t = t * lax.rsqrt(ss + RMS_EPS) * w
    return t * ct + pltpu.roll(t, LANES - QK_ROPE // 2, 1) * s1 + pltpu.roll(t, QK_ROPE // 2, 1) * s2


def _mla_prep_kernel(a_ref, qaw_ref, kvaw_ref, wq_ref, wk_ref, wv_ref, qnw_ref, knw_ref,
                     ct_ref, s1_ref, s2_ref, q_ref, k_ref, v_ref, *, qscale):
    a = a_ref[...]
    ql = a[:, :Q_LORA]
    kvl = a[:, Q_LORA:Q_LORA + KV_LORA]
    kr = a[:, Q_LORA + KV_LORA:]
    qn = (ql * lax.rsqrt(jnp.mean(ql * ql, axis=-1, keepdims=True) + RMS_EPS) * qaw_ref[...]).astype(bf16)
    kvn = (kvl * lax.rsqrt(jnp.mean(kvl * kvl, axis=-1, keepdims=True) + RMS_EPS) * kvaw_ref[...]).astype(bf16)
    q = _dot(qn, wq_ref[...])
    k = _dot(kvn, wk_ref[...])
    v = _dot(kvn, wv_ref[...])
    ct, s1, s2 = ct_ref[...], s1_ref[...], s2_ref[...]
    lane = lax.broadcasted_iota(jnp.int32, (1, LANES), 1)
    ones_col = (lane == V_DIM).astype(f32)
    for h in range(N_HEADS):
        sl = slice(h * LANES, (h + 1) * LANES)
        qh = _head_norm_rope(q[:, sl], qnw_ref[...], ct, s1, s2) * qscale
        kh = _head_norm_rope(k[:, sl] + kr, knw_ref[...], ct, s1, s2)
        q_ref[h] = qh.astype(bf16)
        k_ref[h] = kh.astype(bf16)
        v_ref[h] = (v[:, sl] + ones_col).T[:VT_ROWS].astype(bf16)


def _mla_prep(a3, qaw, kvaw, wq, wk, wv, qnw, knw, ct, s1, s2):
    B, S, NA = a3.shape
    tm = _tile(S, 512)
    HP = N_HEADS * LANES
    qscale = (QK_DIM ** -0.5) * math.log2(math.e)
    full2 = lambda b, i: (0, 0)
    pos = lambda b, i: (i, 0)
    out_spec = pl.BlockSpec((None, N_HEADS, tm, LANES), lambda b, i: (b, 0, i, 0))
    out_sds = jax.ShapeDtypeStruct((B, N_HEADS, S, LANES), bf16)
    return pl.pallas_call(
        functools.partial(_mla_prep_kernel, qscale=qscale),
        grid=(B, S // tm),
        in_specs=[pl.BlockSpec((None, tm, NA), lambda b, i: (b, i, 0)),
                  pl.BlockSpec((1, Q_LORA), full2), pl.BlockSpec((1, KV_LORA), full2),
                  pl.BlockSpec((Q_LORA, HP), full2), pl.BlockSpec((KV_LORA, HP), full2),
                  pl.BlockSpec((KV_LORA, HP), full2),
                  pl.BlockSpec((1, LANES), full2), pl.BlockSpec((1, LANES), full2),
                  pl.BlockSpec((tm, LANES), pos), pl.BlockSpec((tm, LANES), pos), pl.BlockSpec((tm, LANES), pos)],
        out_specs=[out_spec, out_spec,
                   pl.BlockSpec((None, N_HEADS, None, VT_ROWS, tm), lambda b, i: (b, 0, i, 0, 0))],
        out_shape=[out_sds, out_sds, jax.ShapeDtypeStruct((B, N_HEADS, S // tm, VT_ROWS, tm), bf16)],
        compiler_params=_cp(("parallel", "parallel")),
        name="mla_prep",
    )(a3, qaw, kvaw, wq, wk, wv, qnw, knw, ct, s1, s2)


def _flash_kernel(q_ref, k_ref, vt_ref, o_ref, m_sc, acc_sc, s0_sc, s1_sc, *, unroll, online_max):
    nchunk, _, ck = vt_ref.shape
    q = q_ref[...]
    m_sc[...] = jnp.full(m_sc.shape, -jnp.inf, f32)
    acc_sc[...] = jnp.zeros(acc_sc.shape, f32)
    sbuf = (s0_sc, s1_sc)

    def scores(c, buf):
        start = pl.multiple_of(c * ck, ck)
        buf[...] = lax.dot_general(k_ref[pl.ds(start, ck), :], q, (((1,), (1,)), ((), ())),
                                   preferred_element_type=f32)

    def softmax_pv(c, buf):
        st = buf[...]
        if online_max:
            m = m_sc[...]
            m_new = jnp.maximum(m, jnp.max(st, axis=0, keepdims=True))
            pt = jnp.exp2(st - m_new).astype(bf16)
            acc_sc[0:VT_ROWS, :] = jnp.exp2(m - m_new) * acc_sc[0:VT_ROWS, :] + _dot(vt_ref[c], pt)
            m_sc[...] = m_new
        else:
            acc_sc[0:VT_ROWS, :] += _dot(vt_ref[c], jnp.exp2(st).astype(bf16))

    scores(0, s0_sc)

    def body(i, carry):
        for u in range(unroll):
            c = i * unroll + u
            scores(jnp.minimum(c + 1, nchunk - 1), sbuf[(u + 1) % 2])
            softmax_pv(c, sbuf[u % 2])
        return carry

    lax.fori_loop(0, nchunk // unroll, body, 0)
    acc = acc_sc[...].T
    o_ref[...] = (acc[:, :V_DIM] / acc[:, V_DIM:V_DIM + 1]).astype(o_ref.dtype)


SCORE_BOUND_LIMIT = 90.0


def _flash(q, k, vt, score_bound):
    B, H, S, _ = q.shape
    nchunk, ck = vt.shape[2], vt.shape[4]
    tq = _tile(S, 512)
    unroll = 32 if nchunk % 32 == 0 else (4 if nchunk % 4 == 0 else 2)
    assert nchunk % unroll == 0

    def call(online_max):
        return pl.pallas_call(
            functools.partial(_flash_kernel, unroll=unroll, online_max=online_max),
            grid=(B, H, S // tq),
            in_specs=[pl.BlockSpec((None, None, tq, LANES), lambda b, h, i: (b, h, i, 0)),
                      pl.BlockSpec((None, None, S, LANES), lambda b, h, i: (b, h, 0, 0)),
                      pl.BlockSpec((None, None, nchunk, VT_ROWS, ck), lambda b, h, i: (b, h, 0, 0, 0))],
            out_specs=pl.BlockSpec((None, None, tq, V_DIM), lambda b, h, i: (b, h, i, 0)),
            out_shape=jax.ShapeDtypeStruct((B, H, S, V_DIM), bf16),
            scratch_shapes=[pltpu.VMEM((1, tq), f32), pltpu.VMEM((LANES, tq), f32),
                            pltpu.VMEM((ck, tq), f32), pltpu.VMEM((ck, tq), f32)],
            compiler_params=_cp(("parallel", "parallel", "parallel")),
            name="flash_attn_online" if online_max else "flash_attn_bounded")

    return lax.cond(score_bound <= SCORE_BOUND_LIMIT, call(False), call(True), q, k, vt)


def _short_conv_kernel(u_ref, prev_ref, next_ref, w_ref, b_ref, o_ref):
    jb = pl.program_id(1)
    last = pl.num_programs(1) - 1
    nplanes, nh, _ = u_ref.shape
    rows = lax.broadcasted_iota(jnp.int32, (nh, 1), 0)
    prev = prev_ref[...].astype(f32)
    nxt = next_ref[...].astype(f32)
    prev_wrap = jnp.where(rows == 0, 0.0, pltpu.roll(prev, 1, 0))
    next_wrap = jnp.where(rows == nh - 1, 0.0, pltpu.roll(nxt, nh - 1, 0))
    halo_up = jnp.where(jb == 0, prev_wrap, prev)
    halo_dn = jnp.where(jb == last, next_wrap, nxt)
    w = w_ref[...]
    bias = b_ref[...]
    for p in range(nplanes):
        up = halo_up if p == 0 else u_ref[p - 1].astype(f32)
        dn = halo_dn if p == nplanes - 1 else u_ref[p + 1].astype(f32)
        o_ref[p] = (bias + up * w[0:1, :] + u_ref[p].astype(f32) * w[1:2, :] + dn * w[2:3, :]).astype(o_ref.dtype)


def _short_conv(ut, w, b):
    B, P, nh, CT = ut.shape
    tc = _tile(CT, 1024)
    J = 8
    blk = pl.BlockSpec((None, J, nh, tc), lambda bi, j, c: (bi, j, 0, c))
    halo = lambda off: pl.BlockSpec((None, None, nh, tc), lambda bi, j, c: (bi, (j * J + off + P) % P, 0, c))
    return pl.pallas_call(
        _short_conv_kernel,
        grid=(B, P // J, CT // tc),
        in_specs=[blk, halo(-1), halo(J),
                  pl.BlockSpec((3, tc), lambda bi, j, c: (0, c)),
                  pl.BlockSpec((1, tc), lambda bi, j, c: (0, c))],
        out_specs=blk,
        out_shape=jax.ShapeDtypeStruct((B, P, nh, CT), bf16),
        compiler_params=_cp(("parallel", "parallel", "parallel")),
        name="short_conv",
    )(ut, ut, ut, w, b)


def _filter_kernel(feat_ref, tau_ref, w1_ref, b1_ref, f1_ref, w2_ref, b2_ref, f2_ref, w3_ref, dl_ref,
                   h_ref, l1_ref):
    i = pl.program_id(0)
    h = jnp.sin(f1_ref[...] * (_dot_hi(feat_ref[...], w1_ref[...]) + b1_ref[...]))
    h = jnp.sin(f2_ref[...] * (_dot_hi(h, w2_ref[...]) + b2_ref[...]))
    h = _dot_hi(h, w3_ref[...])
    h = h * jnp.exp(-tau_ref[...] * dl_ref[...])
    h_ref[...] = h

    @pl.when(i == 0)
    def _():
        l1_ref[...] = jnp.zeros(l1_ref.shape, f32)

    l1_ref[...] += jnp.sum(jnp.abs(h), axis=0, keepdims=True)


def _filters(feats, tau, w1, b1, f1, w2, b2, f2, w3, dl):
    S = feats.shape[0]
    CO = w3.shape[1]
    tm = _tile(S, 512)
    full = lambda i: (0, 0)
    row = lambda i: (i, 0)
    return pl.pallas_call(
        _filter_kernel,
        grid=(S // tm,),
        in_specs=[pl.BlockSpec((tm, LANES), row), pl.BlockSpec((tm, 1), row),
                  pl.BlockSpec((LANES, LANES), full), pl.BlockSpec((1, LANES), full), pl.BlockSpec((1, LANES), full),
                  pl.BlockSpec((LANES, LANES), full), pl.BlockSpec((1, LANES), full), pl.BlockSpec((1, LANES), full),
                  pl.BlockSpec((LANES, CO), full), pl.BlockSpec((1, CO), full)],
        out_specs=[pl.BlockSpec((tm, CO), row), pl.BlockSpec((1, CO), full)],
        out_shape=[jax.ShapeDtypeStruct((S, CO), f32), jax.ShapeDtypeStruct((1, CO), f32)],
        compiler_params=_cp(("arbitrary",)),
        name="hyena_filters",
    )(feats, tau, w1, b1, f1, w2, b2, f2, w3, dl)


def _pack_pair(re, im):
    hi = pltpu.bitcast(re.astype(bf16).astype(f32), jnp.uint32) & jnp.uint32(0xFFFF0000)
    lo = pltpu.bitcast(im.astype(bf16).astype(f32), jnp.uint32) >> 16
    return hi | lo


def _unpack_pair(w):
    re = pltpu.bitcast(w & jnp.uint32(0xFFFF0000), f32)
    im = pltpu.bitcast(w << 16, f32)
    return re.astype(bf16), im.astype(bf16)


def _dft_a_kernel(x_ref, m_ref, tc_ref, ts_ref, a_ref, *, packed):
    n1f = m_ref.shape[0] // 2
    m = m_ref[...]
    for j in range(tc_ref.shape[0]):
        if packed:
            pr = _dot(m, x_ref[0, j].astype(bf16))
            pi = _dot(m, x_ref[1, j].astype(bf16))
            ar = pr[:n1f] + pi[n1f:]
            ai = pi[:n1f] - pr[n1f:]
        else:
            p = _dot(m, x_ref[j].astype(bf16))
            ar = p[:n1f]
            ai = -p[n1f:]
        tc, ts = tc_ref[j], ts_ref[j]
        a_ref[:, j, :] = _pack_pair(ar * tc + ai * ts, ai * tc - ar * ts)


def _dft_a(x4, col_blk, C, m1, twc, tws, packed):
    n1f = m1.shape[0] // 2
    nh = m1.shape[1]
    J = SUBLANES
    cb = _tile(C, 1024)
    if packed:
        x_spec = pl.BlockSpec((2, J, nh, cb), lambda j, c: (0, j, 0, col_blk + c))
    else:
        x_spec = pl.BlockSpec((J, nh, cb), lambda j, c: (j, 0, col_blk + c))
    return pl.pallas_call(
        functools.partial(_dft_a_kernel, packed=packed),
        grid=(DFT_MINOR // J, C // cb),
        in_specs=[x_spec,
                  pl.BlockSpec((2 * n1f, nh), lambda j, c: (0, 0)),
                  pl.BlockSpec((J, n1f, 1), lambda j, c: (j, 0, 0)),
                  pl.BlockSpec((J, n1f, 1), lambda j, c: (j, 0, 0))],
        out_specs=pl.BlockSpec((n1f, J, cb), lambda j, c: (0, j, c)),
        out_shape=jax.ShapeDtypeStruct((n1f, DFT_MINOR, C), jnp.uint32),
        compiler_params=_cp(("parallel", "parallel")),
        name="dft_stage_a",
    )(x4, m1, twc, tws)


def _dft_bf_kernel(a_ref, m2_ref, inv_ref, h_ref):
    C = a_ref.shape[-1]
    for j in range(a_ref.shape[0]):
        a = jnp.concatenate(_unpack_pair(a_ref[j]), axis=0)
        z = _dot(m2_ref[...], a) * inv_ref[...]
        h_ref[j] = z.reshape(2, DFT_MINOR, C).astype(h_ref.dtype)


def _dft_b_filter(a3, m2f, inv_l1):
    n1f, _, C = a3.shape
    J = DFT_STEP_PLANES
    return pl.pallas_call(
        _dft_bf_kernel,
        grid=(n1f // J,),
        in_specs=[pl.BlockSpec((J, DFT_MINOR, C), lambda k: (k, 0, 0)),
                  pl.BlockSpec((2 * DFT_MINOR, 2 * DFT_MINOR), lambda k: (0, 0)),
                  pl.BlockSpec((1, C), lambda k: (0, 0))],
        out_specs=pl.BlockSpec((J, 2, DFT_MINOR, C), lambda k: (k, 0, 0, 0)),
        out_shape=jax.ShapeDtypeStruct((n1f, 2, DFT_MINOR, C), bf16),
        compiler_params=_cp(("parallel",)),
        name="dft_stage_b_filter",
    )(a3, m2f, inv_l1)


def _dft_b_kernel(a_ref, h_ref, m2f_ref, m2i_ref, tc_ref, ts_ref, b_ref):
    for j in range(a_ref.shape[0]):
        a = jnp.concatenate(_unpack_pair(a_ref[j]), axis=0)
        z = _dot(m2f_ref[...], a)
        zr, zi = z[:DFT_MINOR], z[DFT_MINOR:]
        hr, hi = h_ref[j, 0].astype(f32), h_ref[j, 1].astype(f32)
        y = jnp.concatenate([zr * hr - zi * hi, zr * hi + zi * hr], axis=0).astype(bf16)
        w = _dot(m2i_ref[...], y)
        wr, wi = w[:DFT_MINOR], w[DFT_MINOR:]
        tc, ts = tc_ref[j], ts_ref[j]
        b_ref[:, j, :] = _pack_pair(wr * tc - wi * ts, wr * ts + wi * tc)


def _dft_b(a3, hspec, order, m2f, m2i, twc_t, tws_t):
    n1f, _, C = a3.shape
    J = SUBLANES
    return pl.pallas_call(
        _dft_b_kernel,
        grid=(n1f // J,),
        in_specs=[pl.BlockSpec((J, DFT_MINOR, C), lambda k: (k, 0, 0)),
                  pl.BlockSpec((J, 2, DFT_MINOR, C), lambda k: (k, 0, 0, order)),
                  pl.BlockSpec((2 * DFT_MINOR, 2 * DFT_MINOR), lambda k: (0, 0)),
                  pl.BlockSpec((2 * DFT_MINOR, 2 * DFT_MINOR), lambda k: (0, 0)),
                  pl.BlockSpec((J, DFT_MINOR, 1), lambda k: (k, 0, 0)),
                  pl.BlockSpec((J, DFT_MINOR, 1), lambda k: (k, 0, 0))],
        out_specs=pl.BlockSpec((DFT_MINOR, J, C), lambda k: (0, k, 0)),
        out_shape=jax.ShapeDtypeStruct((DFT_MINOR, n1f, C), jnp.uint32),
        compiler_params=_cp(("parallel",)),
        name="dft_stage_b",
    )(a3, hspec, m2f, m2i, twc_t, tws_t)


def _dft_c_kernel(b_ref, g_ref, z_ref, xg_ref, skip_ref, o_ref):
    n1f = b_ref.shape[1]
    nh = n1f // 2
    skip = skip_ref[...]
    for j in range(b_ref.shape[0]):
        bb = jnp.concatenate(_unpack_pair(b_ref[j]), axis=0)
        y = _dot(g_ref[...], bb)
        for bi in range(2):
            z = z_ref[bi, j].astype(f32)
            o_ref[bi, j] = (xg_ref[bi, j].astype(f32) * (y[bi * nh:(bi + 1) * nh] + z * skip)).astype(o_ref.dtype)


def _dft_c(b3, gm, z4, z_blk, xg4, xg_blk, skip):
    _, n1f, C = b3.shape
    nh = n1f // 2
    J = DFT_STEP_PLANES
    return pl.pallas_call(
        _dft_c_kernel,
        grid=(DFT_MINOR // J,),
        in_specs=[pl.BlockSpec((J, n1f, C), lambda j: (j, 0, 0)),
                  pl.BlockSpec((n1f, 2 * n1f), lambda j: (0, 0)),
                  pl.BlockSpec((2, J, nh, C), lambda j: (0, j, 0, z_blk)),
                  pl.BlockSpec((2, J, nh, C), lambda j: (0, j, 0, xg_blk)),
                  pl.BlockSpec((1, C), lambda j: (0, 0))],
        out_specs=pl.BlockSpec((2, J, nh, C), lambda j: (0, j, 0, 0)),
        out_shape=jax.ShapeDtypeStruct((2, DFT_MINOR, nh, C), bf16),
        compiler_params=_cp(("parallel",)),
        name="dft_stage_c",
    )(b3, gm, z4, xg4, skip)


def _dft_tables(S):
    N = 2 * S
    n1f = N // DFT_MINOR
    nh = n1f // 2
    two_pi = 2.0 * math.pi

    def cs(num, den):
        ang = (num % den).astype(f32) * (two_pi / den)
        return jnp.cos(ang), jnp.sin(ang)

    k1 = jnp.arange(n1f, dtype=jnp.int32)
    n1 = jnp.arange(nh, dtype=jnp.int32)
    c1, s1 = cs(k1[:, None] * n1[None, :], n1f)
    m1 = jnp.concatenate([c1, s1], axis=0).astype(bf16)
    n2 = jnp.arange(DFT_MINOR, dtype=jnp.int32)
    twc, tws = cs(n2[:, None] * k1[None, :], N)
    c2, s2 = cs(n2[:, None] * n2[None, :], DFT_MINOR)
    m2f = jnp.concatenate([jnp.concatenate([c2, s2], 1), jnp.concatenate([-s2, c2], 1)], 0).astype(bf16)
    m2i = jnp.concatenate([jnp.concatenate([c2, -s2], 1), jnp.concatenate([s2, c2], 1)], 0).astype(bf16)
    rows = jnp.arange(nh, dtype=jnp.int32) + n1f // 4
    gc, gs = cs(rows[:, None] * k1[None, :], n1f)
    gm = (jnp.concatenate([jnp.concatenate([gc, -gs], 1), jnp.concatenate([gs, gc], 1)], 0) / N).astype(bf16)
    return dict(m1=m1, twc=twc[:, :, None], tws=tws[:, :, None],
                twc_t=twc.T[:, :, None], tws_t=tws.T[:, :, None], m2f=m2f, m2i=m2i, gm=gm)


def _merge_kernel(x_ref, at_ref, hy_ref, g_ref, wa_ref, wh_ref, wo_ref, n2_ref, wr_ref,
                  xo_ref, hn_ref, aff_ref):
    D = x_ref.shape[1]
    g = g_ref[...].astype(f32)
    merged = g[:, :D] * _dot(at_ref[...], wa_ref[...]) + g[:, D:] * _dot(hy_ref[...], wh_ref[...])
    x = x_ref[...] + _dot(merged.astype(bf16), wo_ref[...])
    xo_ref[...] = x
    hn = x * lax.rsqrt(jnp.mean(x * x, axis=-1, keepdims=True) + RMS_EPS) * n2_ref[...]
    hn_ref[...] = hn.astype(bf16)
    logits = _dot_hi(hn, wr_ref[...])
    lane = lax.broadcasted_iota(jnp.int32, logits.shape, 1)
    logits = jnp.where(lane < N_EXPERTS, logits, -jnp.inf)
    e = jnp.exp(logits - jnp.max(logits, axis=-1, keepdims=True))
    aff_ref[...] = e / jnp.sum(e, axis=-1, keepdims=True)


def _merge(x2, attn, hy, g, wa, wh, wo, n2w, wr):
    T, D = x2.shape
    tm = _tile(T, 512)
    row = lambda i: (i, 0)
    full = lambda i: (0, 0)
    return pl.pallas_call(
        _merge_kernel,
        grid=(T // tm,),
        in_specs=[pl.BlockSpec((tm, D), row), pl.BlockSpec((tm, D), row), pl.BlockSpec((tm, D), row),
                  pl.BlockSpec((tm, 2 * D), row),
                  pl.BlockSpec((D, D), full), pl.BlockSpec((D, D), full), pl.BlockSpec((D, D), full),
                  pl.BlockSpec((1, D), full), pl.BlockSpec((D, LANES), full)],
        out_specs=[pl.BlockSpec((tm, D), row), pl.BlockSpec((tm, D), row), pl.BlockSpec((tm, LANES), row)],
        out_shape=[jax.ShapeDtypeStruct((T, D), f32), jax.ShapeDtypeStruct((T, D), bf16),
                   jax.ShapeDtypeStruct((T, LANES), f32)],
        compiler_params=_cp(("parallel",)),
        name="merge_out_router",
    )(x2, attn, hy, g, wa, wh, wo, n2w, wr)


def _thresh_kernel(a_ref, thr_ref, need_ref, *, cap):
    bits = pltpu.bitcast(a_ref[...], jnp.int32)
    R = bits.shape[0]

    def body(i, thr):
        cand = thr | jnp.left_shift(jnp.int32(1), 30 - i)
        cnt = jnp.sum((bits >= cand).astype(f32), axis=1, keepdims=True)
        return jnp.where(cnt >= cap, cand, thr)

    thr = lax.fori_loop(0, 31, body, jnp.zeros((R, 1), jnp.int32))
    n_gt = jnp.sum((bits > thr).astype(f32), axis=1, keepdims=True)
    thr_ref[...] = jnp.broadcast_to(pltpu.bitcast(thr, f32), thr_ref.shape)
    need_ref[...] = jnp.broadcast_to(cap - n_gt, need_ref.shape)


def _thresholds(aff_t, cap):
    R, S = aff_t.shape
    return pl.pallas_call(
        functools.partial(_thresh_kernel, cap=float(cap)),
        out_shape=[jax.ShapeDtypeStruct((R, LANES), f32), jax.ShapeDtypeStruct((R, LANES), f32)],
        compiler_params=_cp(None),
        name="route_threshold",
    )(aff_t)


def _route_kernel(aff_ref, thr_ref, need_ref, ltri_ref, slab_ref, sel_ref, pos_ref, base_ref, ceq_sc, csel_sc):
    k = pl.program_id(1)

    @pl.when(k == 0)
    def _():
        ceq_sc[...] = jnp.zeros(ceq_sc.shape, f32)
        csel_sc[...] = jnp.zeros(csel_sc.shape, f32)

    a = aff_ref[...]
    thr = thr_ref[...]
    gt = a > thr
    eq = a == thr
    eqf = eq.astype(f32)
    ltri = ltri_ref[...]
    eq_rank = ceq_sc[...] + _dot(ltri, eqf.astype(bf16))
    sel = jnp.logical_or(gt, jnp.logical_and(eq, eq_rank < need_ref[...]))
    self_ = sel.astype(f32)
    pos = csel_sc[...] + _dot(ltri, self_.astype(bf16))
    sel_ref[...] = self_
    pos_ref[...] = pos
    base_ref[...] = csel_sc[...] + _dot(slab_ref[...], self_.astype(bf16))
    ceq_sc[...] += jnp.sum(eqf, axis=0, keepdims=True)
    csel_sc[...] += jnp.sum(self_, axis=0, keepdims=True)


def _route(aff3, thr3, need3):
    B, S, _ = aff3.shape
    tk = _tile(S, 1024)
    ltri = (jnp.arange(tk)[:, None] > jnp.arange(tk)[None, :]).astype(bf16)
    tok = pl.BlockSpec((None, tk, LANES), lambda b, k: (b, k, 0))
    per_b = pl.BlockSpec((None, 1, LANES), lambda b, k: (b, 0, 0))
    return pl.pallas_call(
        _route_kernel,
        grid=(B, S // tk),
        in_specs=[tok, per_b, per_b, pl.BlockSpec((tk, tk), lambda b, k: (0, 0)),
                  pl.BlockSpec((tk // LANES, tk), lambda b, k: (0, 0))],
        out_specs=[tok, tok, pl.BlockSpec((None, tk // LANES, LANES), lambda b, k: (b, k, 0))],
        out_shape=[jax.ShapeDtypeStruct((B, S, LANES), f32), jax.ShapeDtypeStruct((B, S, LANES), f32),
                   jax.ShapeDtypeStruct((B, S // LANES, LANES), f32)],
        scratch_shapes=[pltpu.VMEM((1, LANES), f32), pltpu.VMEM((1, LANES), f32)],
        compiler_params=_cp(("parallel", "arbitrary")),
        name="route_positions",
    )(aff3, thr3, need3, ltri, ltri[::LANES])


def _gather_kernel(base_sm, hn_ref, pos_ref, sel_ref, gate_ref, xs_ref, gs_ref, acc_sc, gacc_sc, *, nsub, nsub_tot):
    b, e, k = pl.program_id(0), pl.program_id(1), pl.program_id(2)
    cap = xs_ref.shape[0]

    @pl.when(k == 0)
    def _():
        acc_sc[...] = jnp.zeros(acc_sc.shape, f32)
        gacc_sc[...] = jnp.zeros(gacc_sc.shape, f32)

    rows = lax.broadcasted_iota(jnp.int32, (GATHER_WIN, LANES), 0).astype(f32)
    for j in range(nsub):
        base = base_sm[(b * N_EXPERTS + e) * nsub_tot + k * nsub + j]
        start = pl.multiple_of((base // SUBLANES) * SUBLANES, SUBLANES)
        sl = slice(j * LANES, (j + 1) * LANES)
        rel = pos_ref[:, sl] - start.astype(f32)
        hit = jnp.logical_and(rows == rel, sel_ref[:, sl] > 0.0)
        oht = jnp.where(hit, 1.0, 0.0)
        acc_sc[pl.ds(start, GATHER_WIN), :] += _dot(oht.astype(bf16), hn_ref[sl, :])
        gacc_sc[pl.ds(start, GATHER_WIN), :] += jnp.sum(oht * gate_ref[:, sl], axis=1, keepdims=True)

    @pl.when(k == pl.num_programs(2) - 1)
    def _():
        xs_ref[...] = acc_sc[0:cap, :].astype(xs_ref.dtype)
        gs_ref[...] = gacc_sc[0:cap, :]


def _gather(base_flat, hn3, pos_t, sel_t, gate_t, cap):
    B, S, D = hn3.shape
    tk = _tile(S, 1024)
    nsub = tk // LANES
    row = pl.BlockSpec((None, None, 1, tk), lambda b, e, k, base: (b, e, 0, k))
    return pl.pallas_call(
        functools.partial(_gather_kernel, nsub=nsub, nsub_tot=S // LANES),
        grid_spec=pltpu.PrefetchScalarGridSpec(
            num_scalar_prefetch=1,
            grid=(B, N_EXPERTS, S // tk),
            in_specs=[pl.BlockSpec((None, tk, D), lambda b, e, k, base: (b, k, 0)), row, row, row],
            out_specs=[pl.BlockSpec((None, None, cap, D), lambda b, e, k, base: (b, e, 0, 0)),
                       pl.BlockSpec((None, None, cap, 1), lambda b, e, k, base: (b, e, 0, 0))],
            scratch_shapes=[pltpu.VMEM((cap + GATHER_WIN, D), f32), pltpu.VMEM((cap + GATHER_WIN, 1), f32)]),
        out_shape=[jax.ShapeDtypeStruct((B, N_EXPERTS, cap, D), bf16),
                   jax.ShapeDtypeStruct((B, N_EXPERTS, cap, 1), f32)],
        compiler_params=_cp(("parallel", "parallel", "arbitrary")),
        name="moe_gather",
    )(base_flat, hn3, pos_t, sel_t, gate_t)


def _ffn_kernel(xs_ref, gs_ref, wg_ref, wu_ref, wd_ref, o_ref):
    xs = xs_ref[...]
    a = _dot(xs, wg_ref[...])
    u = _dot(xs, wu_ref[...])
    hmid = (a * jax.nn.sigmoid(a) * u).astype(bf16)
    o_ref[...] = (_dot(hmid, wd_ref[...]) * gs_ref[...]).astype(o_ref.dtype)


def _ffn(xs, gs, wg, wu, wd):
    B, E, cap, D = xs.shape
    F = wg.shape[-1]
    tm = _tile(cap, 512)
    return pl.pallas_call(
        _ffn_kernel,
        grid=(E, B, cap // tm),
        in_specs=[pl.BlockSpec((None, None, tm, D), lambda e, b, i: (b, e, i, 0)),
                  pl.BlockSpec((None, None, tm, 1), lambda e, b, i: (b, e, i, 0)),
                  pl.BlockSpec((None, D, F), lambda e, b, i: (e, 0, 0)),
                  pl.BlockSpec((None, D, F), lambda e, b, i: (e, 0, 0)),
                  pl.BlockSpec((None, F, D), lambda e, b, i: (e, 0, 0))],
        out_specs=pl.BlockSpec((None, None, tm, D), lambda e, b, i: (b, e, i, 0)),
        out_shape=jax.ShapeDtypeStruct((B, E, cap, D), bf16),
        compiler_params=_cp(("parallel", "parallel", "parallel")),
        name="expert_ffn",
    )(xs, gs, wg, wu, wd)


def _scatter_kernel(base_sm, x_ref, pos_ref, sel_ref, o_hbm, out_ref, buf, sem, *, nsub_tot, cap):
    g = pl.program_id(0)

    def window(step, e):
        b, k = step // nsub_tot, step % nsub_tot
        base = base_sm[(b * N_EXPERTS + e) * nsub_tot + k]
        start = jnp.minimum((base // SUBLANES) * SUBLANES, cap - GATHER_WIN)
        start = pl.multiple_of(start, SUBLANES)
        slot = step % 2
        return start, pltpu.make_async_copy(o_hbm.at[b, e, pl.ds(start, GATHER_WIN), :],
                                            buf.at[slot, e], sem.at[slot, e])

    def fetch(step):
        for e in range(N_EXPERTS):
            window(step, e)[1].start()

    @pl.when(g == 0)
    def _():
        fetch(g)

    @pl.when(g + 1 < pl.num_programs(0))
    def _():
        fetch(g + 1)

    acc = x_ref[...]
    pos = pos_ref[...]
    sel = sel_ref[...]
    lanes = lax.broadcasted_iota(jnp.int32, (LANES, GATHER_WIN), 1).astype(f32)
    starts = []
    for e in range(N_EXPERTS):
        start, cp = window(g, e)
        cp.wait()
        starts.append(start)
    for e in range(N_EXPERTS):
        start = starts[e]
        rel = pos[:, e:e + 1] - start.astype(f32)
        hit = jnp.logical_and(lanes == rel, sel[:, e:e + 1] > 0.0)
        acc = acc + _dot(jnp.where(hit, 1.0, 0.0).astype(bf16), buf[g % 2, e])
    out_ref[...] = acc


def _scatter(base_flat, x3, pos3, sel3, o4):
    B, S, D = x3.shape
    cap = o4.shape[2]
    nsub = S // LANES
    assert cap >= GATHER_WIN
    tok = lambda g, base: (g // nsub, g % nsub, 0)
    return pl.pallas_call(
        functools.partial(_scatter_kernel, nsub_tot=nsub, cap=cap),
        grid_spec=pltpu.PrefetchScalarGridSpec(
            num_scalar_prefetch=1,
            grid=(B * nsub,),
            in_specs=[pl.BlockSpec((None, LANES, D), tok), pl.BlockSpec((None, LANES, LANES), tok),
                      pl.BlockSpec((None, LANES, LANES), tok), pl.BlockSpec(memory_space=pl.ANY)],
            out_specs=pl.BlockSpec((None, LANES, D), tok),
            scratch_shapes=[pltpu.VMEM((2, N_EXPERTS, GATHER_WIN, D), bf16),
                            pltpu.SemaphoreType.DMA((2, N_EXPERTS))]),
        out_shape=jax.ShapeDtypeStruct((B, S, D), f32),
        compiler_params=_cp(("arbitrary",)),
        name="moe_scatter",
    )(base_flat, x3, pos3, sel3, o4)


def _pad_last(a, n):
    return jnp.pad(a, [(0, 0)] * (a.ndim - 1) + [(0, n - a.shape[-1])])


def _rope_tables(S):
    pos = jnp.arange(S, dtype=f32)
    inv_freq = ROPE_THETA ** (-jnp.arange(0, QK_ROPE, 2, dtype=f32) / QK_ROPE)
    ang = pos[:, None] * inv_freq[None, :]
    cos, sin = jnp.cos(ang), jnp.sin(ang)
    half = QK_ROPE // 2
    z_lo = jnp.zeros((S, QK_NOPE), f32)
    z_hi = jnp.zeros((S, LANES - QK_DIM), f32)
    z_half = jnp.zeros((S, half), f32)
    ct = jnp.concatenate([jnp.ones((S, QK_NOPE), f32), cos, cos, z_hi], axis=1)
    s1 = jnp.concatenate([z_lo, -sin, z_half, z_hi], axis=1)
    s2 = jnp.concatenate([z_lo, z_half, sin, z_hi], axis=1)
    return ct, s1, s2


def _hyena_positions(L):
    pos = jnp.arange(L, dtype=f32)
    t = pos / L
    bands = (FILT_EMB - 1) // 2
    f = jnp.linspace(1e-4, bands - 1, bands, dtype=f32)
    w = 2.0 * math.pi * t[:, None] * f[None, :]
    feats = jnp.concatenate([t[:, None], jnp.cos(w), jnp.sin(w)], axis=-1)
    tau = jnp.abs(pos - (L // 2)) / max(L // 2, 1)
    max_decay = math.log(DECAY_TARGET) / FAST_DECAY
    min_decay = math.log(DECAY_TARGET) / SLOW_DECAY
    deltas = jnp.abs(jnp.linspace(min_decay, max_decay, HY_W, dtype=f32))
    planes = lambda t: t.reshape(L // DFT_MINOR, DFT_MINOR, -1).transpose(1, 0, 2).reshape(L, -1)
    return planes(_pad_last(feats, LANES)), planes(tau[:, None]), jnp.tile(deltas, HY_ORDER)[None, :]


def _prep_weights(p):
    L = p["w_in"].shape[0]
    D = p["w_in"].shape[1]
    w_in = p["w_in"]
    kr = jnp.zeros((L, D, LANES), f32).at[:, :, QK_NOPE:QK_DIM].set(w_in[:, :, COL_KROPE:COL_HYENA])
    wq = _pad_last(p["w_q_b"].reshape(L, Q_LORA, N_HEADS, QK_DIM), LANES).reshape(L, Q_LORA, N_HEADS * LANES)
    wkv = p["w_kv_b"].reshape(L, KV_LORA, N_HEADS, QK_NOPE + V_DIM)
    wk = _pad_last(wkv[..., :QK_NOPE], LANES).reshape(L, KV_LORA, N_HEADS * LANES)
    wv = _pad_last(wkv[..., QK_NOPE:], LANES).reshape(L, KV_LORA, N_HEADS * LANES)
    padh = LANES - FILT_HID
    return dict(
        norm1_w=p["norm1_w"][:, None, :],
        wa=jnp.concatenate([w_in[:, :, COL_Q:COL_KROPE], kr], axis=-1).astype(bf16),
        wu=w_in[:, :, COL_HYENA:COL_GATE].astype(bf16),
        wg=w_in[:, :, COL_GATE:].astype(bf16),
        qaw=p["q_a_norm_w"][:, None, :], kvaw=p["kv_a_norm_w"][:, None, :],
        wq=wq.astype(bf16), wk=wk.astype(bf16), wv=wv.astype(bf16),
        qnw=_pad_last(p["q_norm_w"], LANES)[:, None, :], knw=_pad_last(p["k_norm_w"], LANES)[:, None, :],
        scw=p["short_conv_w"], scb=p["short_conv_b"][:, None, :],
        fw1=jnp.pad(p["filt_w1"], ((0, 0), (0, LANES - FILT_EMB), (0, padh))),
        fb1=_pad_last(p["filt_b1"], LANES)[:, None, :], ff1=_pad_last(p["filt_freq1"], LANES)[:, None, :],
        fw2=jnp.pad(p["filt_w2"], ((0, 0), (0, padh), (0, padh))),
        fb2=_pad_last(p["filt_b2"], LANES)[:, None, :], ff2=_pad_last(p["filt_freq2"], LANES)[:, None, :],
        fw3=jnp.pad(p["filt_w3"], ((0, 0), (0, padh), (0, 0))),
        skip=p["filt_skip"],
        w_attn=p["w_attn_branch"].astype(bf16), w_hy=p["w_hyena_branch"].astype(bf16),
        w_out=p["w_out"].astype(bf16),
        norm2_w=p["norm2_w"][:, None, :], wr=_pad_last(p["router_w"], LANES),
        ewg=p["expert_w_gate"].astype(bf16), ewu=p["expert_w_up"].astype(bf16),
        ewd=p["expert_w_down"].astype(bf16),
    )


def _layer(x, w, consts):
    B, S, D = x.shape
    T = B * S
    E = N_EXPERTS
    cap = max(1, EC_FACTOR * S // E)
    nh = S // DFT_MINOR

    a, u, g = _in_proj(x.reshape(T, D), w["norm1_w"], w["wa"], w["wu"], w["wg"])

    q, k, v = _mla_prep(a.reshape(B, S, -1), w["qaw"], w["kvaw"], w["wq"], w["wk"], w["wv"],
                        w["qnw"], w["knw"], consts["ct"], consts["s1"], consts["s2"])
    score_bound = (QK_DIM * (QK_DIM ** -0.5) * math.log2(math.e) * 1.05
                   * jnp.max(jnp.abs(w["qnw"])) * jnp.max(jnp.abs(w["knw"])))
    attn = _flash(q, k, v, score_bound)
    attn = attn.transpose(0, 2, 1, 3).reshape(T, N_HEADS * V_DIM)

    ut = u.reshape(B, nh, DFT_MINOR, -1).transpose(0, 2, 1, 3)
    uct = _short_conv(ut, w["scw"], w["scb"])
    h_un, l1 = _filters(consts["feats"], consts["tau"], w["fw1"], w["fb1"], w["ff1"], w["fw2"], w["fb2"],
                        w["ff2"], w["fw3"], consts["deltas"])
    tb = consts["dft"]
    ha = _dft_a(h_un.reshape(DFT_MINOR, nh, HY_ORDER * HY_W), 0, HY_ORDER * HY_W,
                tb["m1"], tb["twc"], tb["tws"], packed=False)
    hspec = _dft_b_filter(ha, tb["m2f"], 1.0 / l1)
    z4, z_blk = uct, 0
    for n in range(HY_ORDER):
        za = _dft_a(z4, z_blk, HY_W, tb["m1"], tb["twc"], tb["tws"], packed=True)
        zb = _dft_b(za, hspec, n, tb["m2f"], tb["m2i"], tb["twc_t"], tb["tws_t"])
        z4 = _dft_c(zb, tb["gm"], z4, z_blk, uct, n + 1, w["skip"][n:n + 1])
        z_blk = 0
    hy = z4.transpose(0, 2, 1, 3).reshape(T, HY_W)

    x2, hn, aff = _merge(x.reshape(T, D), attn, hy, g, w["w_attn"], w["w_hy"], w["w_out"], w["norm2_w"], w["wr"])

    aff3 = aff.reshape(B, S, LANES)
    aff_t = aff3[:, :, :E].transpose(0, 2, 1)
    thr, need = _thresholds(aff_t.reshape(B * E, S), cap)
    pad_inf = jnp.full((B, LANES - E), jnp.inf, f32)
    thr3 = jnp.concatenate([thr[:, 0].reshape(B, E), pad_inf], axis=1)[:, None, :]
    need3 = _pad_last(need[:, 0].reshape(B, E), LANES)[:, None, :]
    sel, pos, base = _route(aff3, thr3, need3)
    base_flat = base[:, :, :E].transpose(0, 2, 1).reshape(-1).astype(jnp.int32)
    to_rows = lambda t: t[:, :, :E].transpose(0, 2, 1)[:, :, None, :]
    xs, gs = _gather(base_flat, hn.reshape(B, S, D), to_rows(pos), to_rows(sel), aff_t[:, :, None, :], cap)
    o = _ffn(xs, gs, w["ewg"], w["ewu"], w["ewd"])
    return _scatter(base_flat, x2.reshape(B, S, D), pos, sel, o)


def kernel(x, norm1_w, w_in, q_a_norm_w, w_q_b, kv_a_norm_w, w_kv_b, q_norm_w, k_norm_w, short_conv_w,
           short_conv_b, filt_w1, filt_b1, filt_freq1, filt_w2, filt_b2, filt_freq2, filt_w3, filt_skip,
           w_attn_branch, w_hyena_branch, w_out, norm2_w, router_w, expert_w_gate, expert_w_up, expert_w_down):
    B, S, D = x.shape
    assert S % (2 * DFT_MINOR) == 0 and w_hyena_branch.shape[1] == HY_W
    params = dict(norm1_w=norm1_w, w_in=w_in, q_a_norm_w=q_a_norm_w, w_q_b=w_q_b, kv_a_norm_w=kv_a_norm_w,
                  w_kv_b=w_kv_b, q_norm_w=q_norm_w, k_norm_w=k_norm_w, short_conv_w=short_conv_w,
                  short_conv_b=short_conv_b, filt_w1=filt_w1, filt_b1=filt_b1, filt_freq1=filt_freq1,
                  filt_w2=filt_w2, filt_b2=filt_b2, filt_freq2=filt_freq2, filt_w3=filt_w3, filt_skip=filt_skip,
                  w_attn_branch=w_attn_branch, w_hyena_branch=w_hyena_branch, w_out=w_out, norm2_w=norm2_w,
                  router_w=router_w, expert_w_gate=expert_w_gate, expert_w_up=expert_w_up,
                  expert_w_down=expert_w_down)
    weights = _prep_weights(params)
    ct, s1, s2 = _rope_tables(S)
    feats, tau, deltas = _hyena_positions(S)
    consts = dict(ct=ct, s1=s1, s2=s2, feats=feats, tau=tau, deltas=deltas, dft=_dft_tables(S))

    def body(xc, w):
        return _layer(xc, w, consts), None

    out, _ = lax.scan(body, x, weights)
    return out
```
